```python
import math
import functools
import jax
import jax.numpy as jnp
from jax import lax
import numpy as np

D_MODEL = 2048
BATCH = 4
SEQ = 2048
DEPTH = 4
DEC_BATCH = 32
DEC_SEQ = 8
PAST_LEN = 16384
PAGE_SIZE = 128

N_MOD = 9
D_FF = 5632
DN_HEADS = 8
DN_DK = 128
DN_DV = 128
DN_CONV = 4
DN_CHUNK = 64
DN_QK = DN_HEADS * DN_DK
DN_V = DN_HEADS * DN_DV
DN_CONV_CH = 2 * DN_QK + DN_V
DN_SCALE = DN_DK ** -0.5
SW_HEADS = 16
SW_KV_HEADS = 4
SW_HD = 64
SW_GROUP = SW_HEADS // SW_KV_HEADS
SW_SCALE = SW_HD ** -0.5
WINDOW = 128
SW_BLOCK = 128
N_BUCKETS = 32
MAX_DISTANCE = 128
NORM_EPS = 1e-6
NEG_INF = -1e30
IN_SIZES = (DN_CONV_CH, DN_V, DN_HEADS, DN_HEADS, SW_HEADS * SW_HD, SW_KV_HEADS * SW_HD, SW_KV_HEADS * SW_HD, D_MODEL, D_MODEL)
IN_SPLITS = tuple(int(s) for s in np.cumsum(IN_SIZES)[:-1])
N_IN = sum(IN_SIZES)

kernel_name = "hybrid_gdn_swa_macaron_adaln_step"


def win_len():
    return min(WINDOW, PAST_LEN)


def rmsnorm(x, gain):
    xf = x.astype(jnp.float32)
    y = xf * lax.rsqrt(jnp.mean(xf * xf, axis=-1, keepdims=True) + NORM_EPS)
    return (y * gain.astype(jnp.float32)).astype(x.dtype)


def l2norm(x):
    return x * lax.rsqrt(jnp.sum(x * x, axis=-1, keepdims=True) + NORM_EPS)


def swiglu(h, w_in, w_out):
    gate, up = jnp.split(h @ w_in, 2, axis=-1)
    return (jax.nn.silu(gate) * up) @ w_out


def rel_bucket(dist):
    max_exact = N_BUCKETS // 2
    d = jnp.maximum(dist, 0)
    log_ratio = jnp.log(jnp.maximum(d, 1).astype(jnp.float32) / max_exact) / math.log(MAX_DISTANCE / max_exact)
    large = max_exact + (log_ratio * (N_BUCKETS - max_exact)).astype(jnp.int32)
    return jnp.where(d < max_exact, d, jnp.minimum(large, N_BUCKETS - 1))


def rel_bias_grid(dist, rel_bias):
    b = jnp.moveaxis(rel_bias[rel_bucket(dist)], -1, 0).astype(jnp.float32)
    return b.reshape((SW_KV_HEADS, SW_GROUP) + dist.shape)


def sink_softmax(s, sinks):
    sk = jnp.broadcast_to(sinks.astype(jnp.float32).reshape(SW_KV_HEADS, SW_GROUP, 1, 1), s.shape[:-1] + (1,))
    return jax.nn.softmax(jnp.concatenate([s, sk], axis=-1), axis=-1)[..., :-1]


def causal_conv(x, buf, w):
    L = x.shape[1]
    xp = jnp.concatenate([buf.astype(x.dtype), x], axis=1)
    y = sum(xp[:, i:i + L] * w[i] for i in range(DN_CONV))
    return jax.nn.silu(y), xp[:, -(DN_CONV - 1):]


def gated_delta_rule(q, k, v, g, beta, S0):
    B, L, H, _ = q.shape
    DV = v.shape[-1]
    C = math.gcd(L, DN_CHUNK)
    NC = L // C

    def chunks(t):
        return jnp.moveaxis(t.reshape((B, NC, C, H) + t.shape[3:]), 3, 1)

    q, k, v, g, beta = chunks(q), chunks(k), chunks(v), chunks(g), chunks(beta)
    gc = jnp.cumsum(g, axis=-1)
    idx = jnp.arange(C)
    incl = idx[:, None] >= idx[None, :]
    strict = idx[:, None] > idx[None, :]
    decay = jnp.exp(jnp.where(incl, gc[..., :, None] - gc[..., None, :], -jnp.inf))
    kb = k * beta[..., None]
    lmat = jnp.where(strict, jnp.einsum('bhncd,bhnsd->bhncs', kb, k) * decay, 0.0)
    rhs = jnp.concatenate([v * beta[..., None], kb * jnp.exp(gc)[..., None]], axis=-1)
    sol = lax.linalg.triangular_solve(lmat + jnp.eye(C, dtype=lmat.dtype), rhs, left_side=True, lower=True, unit_diagonal=True)
    u, w = sol[..., :DV], sol[..., DV:]
    a_intra = jnp.einsum('bhncd,bhnsd->bhncs', q, k) * decay
    q_dec = q * jnp.exp(gc)[..., None]
    k_dec = k * jnp.exp(gc[..., -1:] - gc)[..., None]
    g_last = jnp.exp(gc[..., -1])

    def step(S, xs):
        u_c, w_c, q_c, k_c, a_c, d_c = xs
        v_new = u_c - jnp.einsum('bhck,bhkv->bhcv', w_c, S)
        o_c = jnp.einsum('bhck,bhkv->bhcv', q_c, S) + jnp.einsum('bhcs,bhsv->bhcv', a_c, v_new)
        S = S * d_c[..., None, None] + jnp.einsum('bhck,bhcv->bhkv', k_c, v_new)
        return S, o_c

    xs = tuple(jnp.moveaxis(t, 2, 0) for t in (u, w, q_dec, k_dec, a_intra, g_last))
    S, o = lax.scan(step, S0, xs)
    return o.transpose(1, 0, 3, 2, 4).reshape(B, L, H, DV), S


def swa_prompt(q, k, v, sinks, rel_bias, win):
    B, L = q.shape[:2]
    NB = L // SW_BLOCK
    qb = q.reshape(B, NB, SW_BLOCK, SW_KV_HEADS, SW_GROUP, SW_HD)
    pad = jnp.zeros((B, SW_BLOCK, SW_KV_HEADS, SW_HD), k.dtype)
    kp = jnp.concatenate([pad, k], axis=1).reshape(B, NB + 1, SW_BLOCK, SW_KV_HEADS, SW_HD)
    vp = jnp.concatenate([pad.astype(v.dtype), v], axis=1).reshape(B, NB + 1, SW_BLOCK, SW_KV_HEADS, SW_HD)
    kb = jnp.concatenate([kp[:, :-1], kp[:, 1:]], axis=2)
    vb = jnp.concatenate([vp[:, :-1], vp[:, 1:]], axis=2)
    qi = jnp.arange(SW_BLOCK)[:, None]
    kj = jnp.arange(2 * SW_BLOCK)[None, :]
    dist = qi + SW_BLOCK - kj
    kpos = jnp.arange(NB)[:, None, None] * SW_BLOCK - SW_BLOCK + kj[None]
    valid = (dist >= 0) & (dist <= WINDOW) & (kpos >= 0)
    s = jnp.einsum('bnqkgd,bnskd->bnkgqs', qb, kb, preferred_element_type=jnp.float32) * SW_SCALE
    s = jnp.where(valid[None, :, None, None], s + rel_bias_grid(dist, rel_bias), NEG_INF)
    p = sink_softmax(s, sinks)
    o = jnp.einsum('bnkgqs,bnskd->bnqkgd', p.astype(v.dtype), vb).reshape(B, L, SW_HEADS * SW_HD)
    return o, k[:, -win:], v[:, -win:]


def swa_sample(q, k, v, sinks, rel_bias, k_buf, v_buf):
    Bd, T = q.shape[:2]
    wc = k_buf.shape[1]
    kk = jnp.concatenate([k_buf.astype(k.dtype), k], axis=1)
    vv = jnp.concatenate([v_buf.astype(v.dtype), v], axis=1)
    dist = (jnp.arange(T)[:, None] + wc) - jnp.arange(wc + T)[None, :]
    valid = (dist >= 0) & (dist <= WINDOW)
    qg = q.reshape(Bd, T, SW_KV_HEADS, SW_GROUP, SW_HD)
    s = jnp.einsum('btkgd,bskd->bkgts', qg, kk, preferred_element_type=jnp.float32) * SW_SCALE
    s = jnp.where(valid, s + rel_bias_grid(dist, rel_bias), NEG_INF)
    p = sink_softmax(s, sinks)
    o = jnp.einsum('bkgts,bskd->btkgd', p.astype(vv.dtype), vv).reshape(Bd, T, SW_HEADS * SW_HD)
    return o, kk[:, -wc:], vv[:, -wc:]


def token_mix(h, lw, rel_bias, conv_buf, S0, swa_fn):
    B, L, _ = h.shape
    x_conv, z, b_raw, a_raw, sq, sk, sv, ga, gb = jnp.split(h @ lw['w_in'], IN_SPLITS, axis=-1)
    xc, new_conv = causal_conv(x_conv, conv_buf, lw['dn_conv_w'])
    dq, dk, dv = jnp.split(xc.astype(jnp.float32), [DN_QK, 2 * DN_QK], axis=-1)
    dq = l2norm(dq.reshape(B, L, DN_HEADS, DN_DK)) * DN_SCALE
    dk = l2norm(dk.reshape(B, L, DN_HEADS, DN_DK))
    dv = dv.reshape(B, L, DN_HEADS, DN_DV)
    beta = jax.nn.sigmoid(b_raw.astype(jnp.float32))
    g = -jnp.exp(lw['dn_a_log'].astype(jnp.float32)) * jax.nn.softplus(a_raw.astype(jnp.float32) + lw['dn_dt_bias'].astype(jnp.float32))
    o_a, S = gated_delta_rule(dq, dk, dv, g, beta, S0.astype(jnp.float32))
    o_a = rmsnorm(o_a, lw['dn_out_norm']) * jax.nn.silu(z.reshape(B, L, DN_HEADS, DN_DV).astype(jnp.float32))
    y_a = o_a.reshape(B, L, DN_V).astype(h.dtype) @ lw['w_dn_out']
    sq = rmsnorm(sq.reshape(B, L, SW_HEADS, SW_HD), lw['sw_q_norm'])
    sk = rmsnorm(sk.reshape(B, L, SW_KV_HEADS, SW_HD), lw['sw_k_norm'])
    sv = sv.reshape(B, L, SW_KV_HEADS, SW_HD)
    o_b, kc, vc = swa_fn(sq, sk, sv, lw['sw_sinks'], rel_bias)
    y_b = o_b @ lw['w_sw_out']
    merged = jax.nn.sigmoid(ga) * y_a + jax.nn.sigmoid(gb) * y_b
    return merged @ lw['w_o'], S.astype(S0.dtype), new_conv, kc, vc


def decoder_layer(x, c, lw, rel_bias, conv_buf, S0, swa_fn):
    mod = (jax.nn.silu(c) @ lw['w_ada'] + lw['b_ada'])[:, None, :]
    sh1, sc1, g1, sh2, sc2, g2, sh3, sc3, g3 = jnp.split(mod, N_MOD, axis=-1)
    h = rmsnorm(x, lw['norm_ffn1']) * (1 + sc1) + sh1
    x = x + 0.5 * g1 * swiglu(h, lw['w_ffn1_in'], lw['w_ffn1_out'])
    h = rmsnorm(x, lw['norm_mix']) * (1 + sc2) + sh2
    y, S, new_conv, kc, vc = token_mix(h, lw, rel_bias, conv_buf, S0, swa_fn)
    x = x + g2 * y
    h = rmsnorm(x, lw['norm_ffn2']) * (1 + sc3) + sh3
    x = x + 0.5 * g3 * swiglu(h, lw['w_ffn2_in'], lw['w_ffn2_out'])
    return x, S, new_conv, kc, vc


def setup_inputs(seed: int = 0) -> dict:
    key = jax.random.key(seed)
    ks = jax.random.split(key, 32)
    D = D_MODEL
    wc = win_len()

    def nrm(k, shape, scale):
        return jax.random.normal(k, shape, jnp.float32) * scale

    def gain(k, shape):
        return 1.0 + 0.02 * jax.random.normal(k, shape, jnp.float32)

    dt = jnp.exp(jax.random.uniform(ks[16], (DEPTH, DN_HEADS), jnp.float32, math.log(1e-3), math.log(1e-1)))
    return {
        'x_prompt': nrm(ks[0], (BATCH, SEQ, D), 1.0),
        'x_sample': nrm(ks[1], (DEC_BATCH, DEC_SEQ, D), 1.0),
        'c_prompt': nrm(ks[2], (BATCH, D), 1.0),
        'c_sample': nrm(ks[3], (DEC_BATCH, D), 1.0),
        'state_delta': nrm(ks[4], (DEPTH, DEC_BATCH, DN_HEADS, DN_DK, DN_DV), 0.1),
        'state_conv': nrm(ks[5], (DEPTH, DEC_BATCH, DN_CONV - 1, DN_CONV_CH), 1.0),
        'cache_swa_k': nrm(ks[6], (DEPTH, DEC_BATCH, wc, SW_KV_HEADS, SW_HD), 1.0),
        'cache_swa_v': nrm(ks[7], (DEPTH, DEC_BATCH, wc, SW_KV_HEADS, SW_HD), 1.0),
        'rel_bias': nrm(ks[8], (N_BUCKETS, SW_HEADS), 0.5),
        'w_ada': nrm(ks[9], (DEPTH, D, N_MOD * D), 0.5 * D ** -0.5),
        'b_ada': nrm(ks[10], (DEPTH, N_MOD * D), 0.02),
        'norm_ffn1': gain(ks[11], (DEPTH, D)),
        'w_ffn1_in': nrm(ks[12], (DEPTH, D, 2 * D_FF), D ** -0.5),
        'w_ffn1_out': nrm(ks[13], (DEPTH, D_FF, D), D_FF ** -0.5),
        'norm_mix': gain(ks[14], (DEPTH, D)),
        'w_in': nrm(ks[15], (DEPTH, D, N_IN), D ** -0.5),
        'dn_conv_w': nrm(ks[17], (DEPTH, DN_CONV, DN_CONV_CH), DN_CONV ** -0.5),
        'dn_a_log': jnp.log(jax.random.uniform(ks[18], (DEPTH, DN_HEADS), jnp.float32, 1.0, 16.0)),
        'dn_dt_bias': dt + jnp.log(-jnp.expm1(-dt)),
        'dn_out_norm': gain(ks[19], (DEPTH, DN_DV)),
        'w_dn_out': nrm(ks[20], (DEPTH, DN_V, D), DN_V ** -0.5),
        'sw_q_norm': gain(ks[21], (DEPTH, SW_HD)),
        'sw_k_norm': gain(ks[22], (DEPTH, SW_HD)),
        'sw_sinks': nrm(ks[23], (DEPTH, SW_HEADS), 0.5),
        'w_sw_out': nrm(ks[24], (DEPTH, SW_HEADS * SW_HD, D), (SW_HEADS * SW_HD) ** -0.5),
        'w_o': nrm(ks[25], (DEPTH, D, D), D ** -0.5),
        'norm_ffn2': gain(ks[26], (DEPTH, D)),
        'w_ffn2_in': nrm(ks[27], (DEPTH, D, 2 * D_FF), D ** -0.5),
        'w_ffn2_out': nrm(ks[28], (DEPTH, D_FF, D), D_FF ** -0.5),
    }


def reference(x_prompt, x_sample, c_prompt, c_sample, state_delta, state_conv, cache_swa_k, cache_swa_v, rel_bias, w_ada, b_ada, norm_ffn1, w_ffn1_in, w_ffn1_out, norm_mix, w_in, dn_conv_w, dn_a_log, dn_dt_bias, dn_out_norm, w_dn_out, sw_q_norm, sw_k_norm, sw_sinks, w_sw_out, w_o, norm_ffn2, w_ffn2_in, w_ffn2_out):
    wc = cache_swa_k.shape[2]
    bp = x_prompt.shape[0]
    conv0 = jnp.zeros((bp, DN_CONV - 1, DN_CONV_CH), x_prompt.dtype)
    s0 = jnp.zeros((bp, DN_HEADS, DN_DK, DN_DV), x_prompt.dtype)
    yp, ys = x_prompt, x_sample
    dp, cp, kp, vp = [], [], [], []
    ds, cs, kss, vss = [], [], [], []
    for l in range(DEPTH):
        lw = dict(w_ada=w_ada[l], b_ada=b_ada[l], norm_ffn1=norm_ffn1[l], w_ffn1_in=w_ffn1_in[l], w_ffn1_out=w_ffn1_out[l], norm_mix=norm_mix[l], w_in=w_in[l], dn_conv_w=dn_conv_w[l], dn_a_log=dn_a_log[l], dn_dt_bias=dn_dt_bias[l], dn_out_norm=dn_out_norm[l], w_dn_out=w_dn_out[l], sw_q_norm=sw_q_norm[l], sw_k_norm=sw_k_norm[l], sw_sinks=sw_sinks[l], w_sw_out=w_sw_out[l], w_o=w_o[l], norm_ffn2=norm_ffn2[l], w_ffn2_in=w_ffn2_in[l], w_ffn2_out=w_ffn2_out[l])
        yp, S_p, conv_p, k_p, v_p = decoder_layer(yp, c_prompt, lw, rel_bias, conv0, s0, functools.partial(swa_prompt, win=wc))
        ys, S_s, conv_s, k_s, v_s = decoder_layer(ys, c_sample, lw, rel_bias, state_conv[l], state_delta[l], functools.partial(swa_sample, k_buf=cache_swa_k[l], v_buf=cache_swa_v[l]))
        dp.append(S_p); cp.append(conv_p); kp.append(k_p); vp.append(v_p)
        ds.append(S_s); cs.append(conv_s); kss.append(k_s); vss.append(v_s)
    return (yp, ys, jnp.stack(dp), jnp.stack(cp), jnp.stack(kp), jnp.stack(vp), jnp.stack(ds), jnp.stack(cs), jnp.stack(kss), jnp.stack(vss))
```

```python
import functools
import math

import numpy as np
import jax
import jax.numpy as jnp
from jax import lax
from jax.experimental import pallas as pl
from jax.experimental.pallas import tpu as pltpu

F32 = jnp.float32
BF16 = jnp.bfloat16

NORM_EPS = 1e-6
NEG_INF = -1e30
N_MOD = 9

DN_HEADS = 8
DN_DK = 128
DN_DV = 128
DN_CONV = 4
DN_CHUNK = 64
DN_QK = DN_HEADS * DN_DK
DN_V = DN_HEADS * DN_DV
DN_CONV_CH = 2 * DN_QK + DN_V
DN_SCALE = DN_DK ** -0.5

SW_HEADS = 16
SW_KV_HEADS = 4
SW_HD = 64
SW_GROUP = SW_HEADS // SW_KV_HEADS
SW_SCALE = SW_HD ** -0.5
WINDOW = 128
SW_BLOCK = 128
SW_Q = SW_HEADS * SW_HD
SW_KV = SW_KV_HEADS * SW_HD

N_BUCKETS = 32
MAX_DISTANCE = 128

LANES = 128
SUBLANES = 8
VMEM_LIMIT_BYTES = 56 * 1024 * 1024

COL_CONV = 0
COL_Z = COL_CONV + DN_CONV_CH
COL_GA = COL_Z + DN_V
BD_WIDTH = LANES


def _sigmoid(x):
    return 1.0 / (1.0 + jnp.exp(-x))


def _silu(x):
    return x * _sigmoid(x)


def _softplus(x):
    return jnp.maximum(x, 0.0) + jnp.log1p(jnp.exp(-jnp.abs(x)))


def _dot(a, b, **kw):
    return jnp.dot(a, b, preferred_element_type=F32, **kw)


def _dot_nt(a, b):
    return lax.dot_general(a, b, (((1,), (1,)), ((), ())), preferred_element_type=F32)


def _dot_tn(a, b):
    return lax.dot_general(a, b, (((0,), (0,)), ((), ())), preferred_element_type=F32)


def _params(*semantics):
    return pltpu.CompilerParams(dimension_semantics=semantics, vmem_limit_bytes=VMEM_LIMIT_BYTES)


def _mod_norm(x, gain, sc, sh):
    y = x * lax.rsqrt(jnp.mean(x * x, axis=-1, keepdims=True) + NORM_EPS) * gain
    return y * (1.0 + sc) + sh


def _ada_kernel(c_ref, w_ref, b_ref, o_ref):
    a = _silu(c_ref[...]).astype(BF16)
    o_ref[...] = _dot(a, w_ref[...].astype(BF16)) + b_ref[...]


def _ada(c_all, w_ada, b_ada, tn=1024):
    depth, d, n = w_ada.shape
    rows = c_all.shape[0]
    return pl.pallas_call(
        _ada_kernel,
        grid=(depth, n // tn),
        in_specs=[
            pl.BlockSpec((rows, d), lambda l, j: (0, 0)),
            pl.BlockSpec((None, d, tn), lambda l, j: (l, 0, j)),
            pl.BlockSpec((None, 1, tn), lambda l, j: (l, 0, j)),
        ],
        out_specs=pl.BlockSpec((None, rows, tn), lambda l, j: (l, 0, j)),
        out_shape=jax.ShapeDtypeStruct((depth, rows, n), F32),
        compiler_params=_params("arbitrary", "arbitrary"),
        name="ada",
    )(c_all, w_ada, b_ada.reshape(depth, 1, n))


def _bucket_table():
    max_exact = N_BUCKETS // 2
    d = np.arange(WINDOW + 1)
    ratio = np.log(np.maximum(d, 1).astype(np.float32) / np.float32(max_exact)) / np.float32(math.log(MAX_DISTANCE / max_exact))
    large = max_exact + (ratio.astype(np.float32) * np.float32(N_BUCKETS - max_exact)).astype(np.int32)
    return np.where(d < max_exact, d, np.minimum(large, N_BUCKETS - 1)).astype(np.int32)


def _bucket_map(dist):
    table = _bucket_table()
    valid = (dist >= 0) & (dist <= WINDOW)
    return np.where(valid, table[np.clip(dist, 0, WINDOW)], -1).astype(np.int32)


def _bias_kernel(rb_ref, bm_ref, o_ref):
    bm = bm_ref[...]
    tq = bm.shape[0]
    for g in range(SW_KV_HEADS):
        for j in range(SW_GROUP):
            acc = jnp.full(bm.shape, NEG_INF, F32)
            for b in range(N_BUCKETS):
                acc = jnp.where(bm == b, rb_ref[b, g * SW_GROUP + j], acc)
            o_ref[g, j * tq:(j + 1) * tq, :] = acc


def _bias_grid(rel_bias, bmap):
    tq, tk = bmap.shape
    return pl.pallas_call(
        _bias_kernel,
        in_specs=[
            pl.BlockSpec(memory_space=pltpu.SMEM),
            pl.BlockSpec((tq, tk), lambda: (0, 0)),
        ],
        out_specs=pl.BlockSpec((SW_KV_HEADS, SW_GROUP * tq, tk), lambda: (0, 0, 0)),
        out_shape=jax.ShapeDtypeStruct((SW_KV_HEADS, SW_GROUP * tq, tk), F32),
        name="relbias",
    )(rel_bias, jnp.asarray(bmap))


def _ffn_kernel(x_ref, sh_ref, sc_ref, g_ref, gain_ref, wg_ref, wu_ref, wo_ref, o_ref, h_ref, *, nf):
    f = pl.program_id(1)

    @pl.when(f == 0)
    def _():
        h_ref[...] = _mod_norm(x_ref[...], gain_ref[...], sc_ref[...], sh_ref[...]).astype(BF16)
        o_ref[...] = jnp.zeros_like(o_ref)

    h = h_ref[...]
    gate = _dot(h, wg_ref[...].astype(BF16))
    up = _dot(h, wu_ref[...].astype(BF16))
    a = (_silu(gate) * up).astype(BF16)
    o_ref[...] += _dot(a, wo_ref[...].astype(BF16))

    @pl.when(f == nf - 1)
    def _():
        o_ref[...] = x_ref[...] + (0.5 * g_ref[...]) * o_ref[...]


def _mod_spec(mod, layer, seg, d, tiles_per_row):
    r = mod.shape[2]
    return pl.BlockSpec((None, None, r, d), lambda i, j: (layer, i // tiles_per_row, 0, seg))


def _ffn(x, mod, seg0, layer, gain, w_in, w_out, tm, tf):
    m, d = x.shape
    ff = w_out.shape[1]
    nf = ff // tf
    tiles_per_row = (m // mod.shape[1]) // tm
    return pl.pallas_call(
        functools.partial(_ffn_kernel, nf=nf),
        grid=(m // tm, nf),
        in_specs=[
            pl.BlockSpec((tm, d), lambda i, f: (i, 0), pipeline_mode=pl.Buffered(1)),
            _mod_spec(mod, layer, seg0, d, tiles_per_row),
            _mod_spec(mod, layer, seg0 + 1, d, tiles_per_row),
            _mod_spec(mod, layer, seg0 + 2, d, tiles_per_row),
            pl.BlockSpec((None, 1, d), lambda i, f: (layer, 0, 0)),
            pl.BlockSpec((None, d, tf), lambda i, f: (layer, 0, f)),
            pl.BlockSpec((None, d, tf), lambda i, f: (layer, 0, f + nf)),
            pl.BlockSpec((None, tf, d), lambda i, f: (layer, f, 0)),
        ],
        out_specs=pl.BlockSpec((tm, d), lambda i, f: (i, 0)),
        out_shape=jax.ShapeDtypeStruct((m, d), F32),
        scratch_shapes=[pltpu.VMEM((tm, d), BF16)],
        compiler_params=_params("arbitrary", "arbitrary"),
        name="ffn",
    )(x, mod, mod, mod, gain, w_in, w_in, w_out)


def _inproj_kernel(x_ref, sh_ref, sc_ref, gain_ref, w_ref, o_ref, h_ref):
    @pl.when(pl.program_id(1) == 0)
    def _():
        h_ref[...] = _mod_norm(x_ref[...], gain_ref[...], sc_ref[...], sh_ref[...]).astype(BF16)

    o_ref[...] = _dot(h_ref[...], w_ref[...])


def _inproj(x, mod, layer, gain, w_r, tm, tn):
    m, d = x.shape
    nr = w_r.shape[2]
    tiles_per_row = (m // mod.shape[1]) // tm
    return pl.pallas_call(
        _inproj_kernel,
        grid=(m // tm, nr // tn),
        in_specs=[
            pl.BlockSpec((tm, d), lambda i, j: (i, 0)),
            _mod_spec(mod, layer, 3, d, tiles_per_row),
            _mod_spec(mod, layer, 4, d, tiles_per_row),
            pl.BlockSpec((None, 1, d), lambda i, j: (layer, 0, 0)),
            pl.BlockSpec((None, d, tn), lambda i, j: (layer, 0, j)),
        ],
        out_specs=pl.BlockSpec((tm, tn), lambda i, j: (i, j)),
        out_shape=jax.ShapeDtypeStruct((m, nr), F32),
        scratch_shapes=[pltpu.VMEM((tm, d), BF16)],
        compiler_params=_params("arbitrary", "arbitrary"),
        name="inproj",
    )(x, mod, mod, gain, w_r)


def _tri_solve(lmat, rhs):
    c = lmat.shape[0]
    nb = c // SUBLANES
    lb = [lmat[SUBLANES * i:SUBLANES * (i + 1), :] for i in range(nb)]
    xb = [rhs[SUBLANES * i:SUBLANES * (i + 1), :] for i in range(nb)]
    for j in range(c - 1):
        bj, rj = divmod(j, SUBLANES)
        row = xb[bj][rj:rj + 1, :]
        for i in range(bj, nb):
            xb[i] = xb[i] - lb[i][:, j:j + 1] * row
    return xb[0] if nb == 1 else jnp.concatenate(xb, axis=0)


def _gdn_kernel(*refs, chunk, has_init):
    if has_init:
        (xc_ref, z_ref, bd_ref, cw_ref, alog_ref, dt_ref, on_ref, conv0_ref, s0_ref,
         o_ref, s_ref, conv_ref, xs_ref) = refs
    else:
        (xc_ref, z_ref, bd_ref, cw_ref, alog_ref, dt_ref, on_ref,
         o_ref, s_ref, conv_ref, xs_ref) = refs
    t = pl.program_id(1)
    nt = pl.num_programs(1)
    c = chunk
    hist = SUBLANES

    @pl.when(t == 0)
    def _():
        xs_ref[0:hist, :] = jnp.zeros((hist, DN_CONV_CH), F32)
        if has_init:
            xs_ref[hist - (DN_CONV - 1):hist, :] = conv0_ref[...]
            s_ref[...] = s0_ref[...]
        else:
            s_ref[...] = jnp.zeros_like(s_ref)

    xs_ref[hist:hist + c, :] = xc_ref[...]
    acc = xs_ref[hist:hist + c, :] * cw_ref[DN_CONV - 1:DN_CONV, :]
    for i in range(DN_CONV - 1):
        off = hist - (DN_CONV - 1) + i
        acc = acc + xs_ref[off:off + c, :] * cw_ref[i:i + 1, :]
    xc = _silu(acc)

    @pl.when(t == nt - 1)
    def _():
        conv_ref[...] = xs_ref[hist + c - (DN_CONV - 1):hist + c, :]

    xs_ref[0:hist, :] = xs_ref[c:c + hist, :]

    bd = bd_ref[...]
    beta_all = _sigmoid(bd)
    g_all = -jnp.exp(alog_ref[...]) * _softplus(bd + dt_ref[...])
    ri = lax.broadcasted_iota(jnp.int32, (c, c), 0)
    ci = lax.broadcasted_iota(jnp.int32, (c, c), 1)
    incl = ri >= ci
    strict = ri > ci
    gc_all = _dot(incl.astype(F32), g_all, precision=lax.Precision.HIGHEST)
    if c < LANES:
        gc_pad = jnp.concatenate([gc_all, jnp.zeros((LANES - c, LANES), F32)], axis=0)
    else:
        gc_pad = gc_all
    gc_t = gc_pad.T

    on = on_ref[...]
    for h in range(DN_HEADS):
        q = xc[:, DN_DK * h:DN_DK * (h + 1)]
        k = xc[:, DN_QK + DN_DK * h:DN_QK + DN_DK * (h + 1)]
        v = xc[:, 2 * DN_QK + DN_DV * h:2 * DN_QK + DN_DV * (h + 1)]
        q = q * lax.rsqrt(jnp.sum(q * q, axis=-1, keepdims=True) + NORM_EPS) * DN_SCALE
        k = k * lax.rsqrt(jnp.sum(k * k, axis=-1, keepdims=True) + NORM_EPS)
        beta = beta_all[:, h:h + 1]
        gcol = gc_all[:, DN_HEADS + h:DN_HEADS + h + 1]
        grow = gc_t[DN_HEADS + h:DN_HEADS + h + 1, 0:c]
        glast = gc_all[c - 1:c, DN_HEADS + h:DN_HEADS + h + 1]
        decay = jnp.exp(jnp.where(incl, gcol - grow, NEG_INF))
        egc = jnp.exp(gcol)
        kb = k * beta
        k16 = k.astype(BF16)
        lmat = jnp.where(strict, _dot_nt(kb.astype(BF16), k16) * decay, 0.0)
        a_intra = _dot_nt(q.astype(BF16), k16) * decay
        sol = _tri_solve(lmat, jnp.concatenate([v * beta, kb * egc], axis=-1))
        u = sol[:, :DN_DV]
        w = sol[:, DN_DV:]
        s = s_ref[h]
        s16 = s.astype(BF16)
        v_new = u - _dot(w.astype(BF16), s16)
        v16 = v_new.astype(BF16)
        o = _dot((q * egc).astype(BF16), s16) + _dot(a_intra.astype(BF16), v16)
        k_dec = k * jnp.exp(glast - gcol)
        s_ref[h] = s * jnp.exp(glast) + _dot_tn(k_dec.astype(BF16), v16)
        o = o * lax.rsqrt(jnp.mean(o * o, axis=-1, keepdims=True) + NORM_EPS) * on
        zh = z_ref[:, DN_DV * h:DN_DV * (h + 1)]
        o_ref[:, DN_DV * h:DN_DV * (h + 1)] = (o * _silu(zh)).astype(o_ref.dtype)


def _gdn(y, batch, layer, conv_w, alog_row, dt_row, out_norm, conv0, s0, chunk, out_dtype):
    m = y.shape[0]
    seq = m // batch
    nt = seq // chunk
    has_init = conv0 is not None
    row = lambda b, t: b * nt + t
    in_specs = [
        pl.BlockSpec((chunk, DN_CONV_CH), lambda b, t: (row(b, t), COL_CONV // DN_CONV_CH)),
        pl.BlockSpec((chunk, DN_V), lambda b, t: (row(b, t), COL_Z // DN_V)),
        pl.BlockSpec((chunk, BD_WIDTH), lambda b, t: (row(b, t), y.shape[1] // BD_WIDTH - 1 - _BD_TAIL_BLOCKS)),
        pl.BlockSpec((None, DN_CONV, DN_CONV_CH), lambda b, t: (layer, 0, 0)),
        pl.BlockSpec((None, 1, LANES), lambda b, t: (layer, 0, 0)),
        pl.BlockSpec((None, 1, LANES), lambda b, t: (layer, 0, 0)),
        pl.BlockSpec((None, 1, DN_DV), lambda b, t: (layer, 0, 0)),
    ]
    args = [y, y, y, conv_w, alog_row, dt_row, out_norm]
    if has_init:
        in_specs += [
            pl.BlockSpec((None, None, DN_CONV - 1, DN_CONV_CH), lambda b, t: (layer, b, 0, 0)),
            pl.BlockSpec((None, None, DN_HEADS, DN_DK, DN_DV), lambda b, t: (layer, b, 0, 0, 0)),
        ]
        args += [conv0, s0]
    return pl.pallas_call(
        functools.partial(_gdn_kernel, chunk=chunk, has_init=has_init),
        grid=(batch, nt),
        in_specs=in_specs,
        out_specs=[
            pl.BlockSpec((chunk, DN_V), lambda b, t: (row(b, t), 0)),
            pl.BlockSpec((None, DN_HEADS, DN_DK, DN_DV), lambda b, t: (b, 0, 0, 0)),
            pl.BlockSpec((None, DN_CONV - 1, DN_CONV_CH), lambda b, t: (b, 0, 0)),
        ],
        out_shape=[
            jax.ShapeDtypeStruct((m, DN_V), out_dtype),
            jax.ShapeDtypeStruct((batch, DN_HEADS, DN_DK, DN_DV), F32),
            jax.ShapeDtypeStruct((batch, DN_CONV - 1, DN_CONV_CH), F32),
        ],
        scratch_shapes=[pltpu.VMEM((chunk + SUBLANES, DN_CONV_CH), F32)],
        compiler_params=_params("arbitrary", "arbitrary"),
        name="gdn",
    )(*args)


def _head_norm(x, gain, heads):
    parts = []
    for h in range(heads):
        xh = x[:, SW_HD * h:SW_HD * (h + 1)]
        parts.append(xh * lax.rsqrt(jnp.mean(xh * xh, axis=-1, keepdims=True) + NORM_EPS) * gain)
    return parts


def _attn_group(qs, keys, vals, biases, sink):
    tq = qs[0].shape[0]
    qg = jnp.concatenate(qs, axis=0).astype(BF16)
    ss = [_dot_nt(qg, k.astype(BF16)) * SW_SCALE + b for k, b in zip(keys, biases)]
    m = sink
    for s in ss:
        m = jnp.maximum(m, jnp.max(s, axis=-1, keepdims=True))
    ps = [jnp.exp(s - m) for s in ss]
    den = jnp.exp(sink - m)
    for p in ps:
        den = den + jnp.sum(p, axis=-1, keepdims=True)
    inv = 1.0 / den
    o = None
    for p, v in zip(ps, vals):
        pv = _dot((p * inv).astype(BF16), v.astype(BF16))
        o = pv if o is None else o + pv
    return [o[tq * j:tq * (j + 1), :] for j in range(SW_GROUP)]


def _swa_prompt_kernel(q_ref, kp_ref, kc_ref, vp_ref, vc_ref, qn_ref, kn_ref, sink_ref, bias_ref,
                       o_ref, kout_ref, vout_ref):
    n = pl.program_id(1)
    nb = pl.num_programs(1)
    qn = _head_norm(q_ref[...], qn_ref[...], SW_HEADS)
    kc = _head_norm(kc_ref[...], kn_ref[...], SW_KV_HEADS)
    kp = _head_norm(kp_ref[...], kn_ref[...], SW_KV_HEADS)
    vc = vc_ref[...]
    vp = vp_ref[...]
    first = n == 0
    outs = []
    for g in range(SW_KV_HEADS):
        bias = bias_ref[g]
        bias_prev = jnp.where(first, NEG_INF, bias[:, :SW_BLOCK])
        outs += _attn_group(
            qn[SW_GROUP * g:SW_GROUP * (g + 1)],
            [kp[g], kc[g]],
            [vp[:, SW_HD * g:SW_HD * (g + 1)], vc[:, SW_HD * g:SW_HD * (g + 1)]],
            [bias_prev, bias[:, SW_BLOCK:]],
            sink_ref[g])
    o_ref[...] = jnp.concatenate(outs, axis=-1).astype(o_ref.dtype)

    @pl.when(n == nb - 1)
    def _():
        kout_ref[...] = jnp.concatenate(kc, axis=-1)
        vout_ref[...] = vc


def _swa_prompt(y, batch, layer, q_norm, k_norm, sink_col, bias, col_q, out_dtype):
    m = y.shape[0]
    seq = m // batch
    nb = seq // SW_BLOCK
    cq = col_q // SW_Q
    ck = (col_q + SW_Q) // SW_KV
    cv = ck + 1
    cur = lambda b, n: b * nb + n
    prev = lambda b, n: b * nb + jnp.maximum(n - 1, 0)
    return pl.pallas_call(
        _swa_prompt_kernel,
        grid=(batch, nb),
        in_specs=[
            pl.BlockSpec((SW_BLOCK, SW_Q), lambda b, n: (cur(b, n), cq)),
            pl.BlockSpec((SW_BLOCK, SW_KV), lambda b, n: (prev(b, n), ck)),
            pl.BlockSpec((SW_BLOCK, SW_KV), lambda b, n: (cur(b, n), ck)),
            pl.BlockSpec((SW_BLOCK, SW_KV), lambda b, n: (prev(b, n), cv)),
            pl.BlockSpec((SW_BLOCK, SW_KV), lambda b, n: (cur(b, n), cv)),
            pl.BlockSpec((None, 1, SW_HD), lambda b, n: (layer, 0, 0)),
            pl.BlockSpec((None, 1, SW_HD), lambda b, n: (layer, 0, 0)),
            pl.BlockSpec((None, SW_KV_HEADS, SW_GROUP * SW_BLOCK, 1), lambda b, n: (layer, 0, 0, 0)),
            pl.BlockSpec((SW_KV_HEADS, SW_GROUP * SW_BLOCK, 2 * SW_BLOCK), lambda b, n: (0, 0, 0)),
        ],
        out_specs=[
            pl.BlockSpec((SW_BLOCK, SW_Q), lambda b, n: (cur(b, n), 0)),
            pl.BlockSpec((None, WINDOW, SW_KV), lambda b, n: (b, 0, 0)),
            pl.BlockSpec((None, WINDOW, SW_KV), lambda b, n: (b, 0, 0)),
        ],
        out_shape=[
            jax.ShapeDtypeStruct((m, SW_Q), out_dtype),
            jax.ShapeDtypeStruct((batch, WINDOW, SW_KV), F32),
            jax.ShapeDtypeStruct((batch, WINDOW, SW_KV), F32),
        ],
        compiler_params=_params("arbitrary", "arbitrary"),
        name="swa_prompt",
    )(y, y, y, y, y, q_norm, k_norm, sink_col, bias)


def _swa_sample_kernel(q_ref, k_ref, v_ref, kbuf_ref, vbuf_ref, qn_ref, kn_ref, sink_ref, bb_ref, bn_ref,
                       o_ref, kout_ref, vout_ref, *, bs):
    t = q_ref.shape[1]
    wc = kbuf_ref.shape[1]
    for i in range(bs):
        qn = _head_norm(q_ref[i], qn_ref[...], SW_HEADS)
        kn = _head_norm(k_ref[i], kn_ref[...], SW_KV_HEADS)
        v = v_ref[i]
        kbuf = kbuf_ref[i]
        vbuf = vbuf_ref[i]
        outs = []
        for g in range(SW_KV_HEADS):
            outs += _attn_group(
                qn[SW_GROUP * g:SW_GROUP * (g + 1)],
                [kbuf[:, SW_HD * g:SW_HD * (g + 1)], kn[g]],
                [vbuf[:, SW_HD * g:SW_HD * (g + 1)], v[:, SW_HD * g:SW_HD * (g + 1)]],
                [bb_ref[g], bn_ref[g]],
                sink_ref[g])
        o_ref[i] = jnp.concatenate(outs, axis=-1).astype(o_ref.dtype)
        kout_ref[i, 0:wc - t, :] = kbuf[t:, :]
        kout_ref[i, wc - t:wc, :] = jnp.concatenate(kn, axis=-1)
        vout_ref[i, 0:wc - t, :] = vbuf[t:, :]
        vout_ref[i, wc - t:wc, :] = v


def _swa_sample(y3, layer, k_buf, v_buf, q_norm, k_norm, sink_col, bias_buf, bias_new, col_q, bs):
    batch, t, _ = y3.shape
    wc = k_buf.shape[2]
    cq = col_q // SW_Q
    ck = (col_q + SW_Q) // SW_KV
    cv = ck + 1
    return pl.pallas_call(
        functools.partial(_swa_sample_kernel, bs=bs),
        grid=(batch // bs,),
        in_specs=[
            pl.BlockSpec((bs, t, SW_Q), lambda b: (b, 0, cq)),
            pl.BlockSpec((bs, t, SW_KV), lambda b: (b, 0, ck)),
            pl.BlockSpec((bs, t, SW_KV), lambda b: (b, 0, cv)),
            pl.BlockSpec((None, bs, wc, SW_KV), lambda b: (layer, b, 0, 0)),
            pl.BlockSpec((None, bs, wc, SW_KV), lambda b: (layer, b, 0, 0)),
            pl.BlockSpec((None, 1, SW_HD), lambda b: (layer, 0, 0)),
            pl.BlockSpec((None, 1, SW_HD), lambda b: (layer, 0, 0)),
            pl.BlockSpec((None, SW_KV_HEADS, SW_GROUP * t, 1), lambda b: (layer, 0, 0, 0)),
            pl.BlockSpec((SW_KV_HEADS, SW_GROUP * t, wc), lambda b: (0, 0, 0)),
            pl.BlockSpec((SW_KV_HEADS, SW_GROUP * t, t), lambda b: (0, 0, 0)),
        ],
        out_specs=[
            pl.BlockSpec((bs, t, SW_Q), lambda b: (b, 0, 0)),
            pl.BlockSpec((bs, wc, SW_KV), lambda b: (b, 0, 0)),
            pl.BlockSpec((bs, wc, SW_KV), lambda b: (b, 0, 0)),
        ],
        out_shape=[
            jax.ShapeDtypeStruct((batch, t, SW_Q), F32),
            jax.ShapeDtypeStruct((batch, wc, SW_KV), F32),
            jax.ShapeDtypeStruct((batch, wc, SW_KV), F32),
        ],
        compiler_params=_params("arbitrary"),
        name="swa_sample",
    )(y3, y3, y3, k_buf, v_buf, q_norm, k_norm, sink_col, bias_buf, bias_new)


def _mixout_kernel(x_ref, oa_ref, ob_ref, ga_ref, gb_ref, g_ref, wdn_ref, wsw_ref, wo_ref, o_ref):
    ya = _dot(oa_ref[...].astype(BF16), wdn_ref[...])
    yb = _dot(ob_ref[...].astype(BF16), wsw_ref[...])
    merged = _sigmoid(ga_ref[...]) * ya + _sigmoid(gb_ref[...]) * yb
    o_ref[...] = x_ref[...] + g_ref[...] * _dot(merged.astype(BF16), wo_ref[...])


def _mixout(x, o_a, o_b, y, mod, layer, w_dn, w_sw, w_o, tm):
    m, d = x.shape
    tiles_per_row = (m // mod.shape[1]) // tm
    cga = COL_GA // d
    const = dict(pipeline_mode=pl.Buffered(1))
    return pl.pallas_call(
        _mixout_kernel,
        grid=(m // tm,),
        in_specs=[
            pl.BlockSpec((tm, d), lambda i: (i, 0)),
            pl.BlockSpec((tm, DN_V), lambda i: (i, 0)),
            pl.BlockSpec((tm, SW_Q), lambda i: (i, 0)),
            pl.BlockSpec((tm, d), lambda i: (i, cga)),
            pl.BlockSpec((tm, d), lambda i: (i, cga + 1)),
            pl.BlockSpec((None, None, mod.shape[2], d), lambda i: (layer, i // tiles_per_row, 0, 5)),
            pl.BlockSpec((None, DN_V, d), lambda i: (layer, 0, 0), **const),
            pl.BlockSpec((None, SW_Q, d), lambda i: (layer, 0, 0), **const),
            pl.BlockSpec((None, d, d), lambda i: (layer, 0, 0), **const),
        ],
        out_specs=pl.BlockSpec((tm, d), lambda i: (i, 0)),
        out_shape=jax.ShapeDtypeStruct((m, d), F32),
        compiler_params=_params("arbitrary"),
        name="mixout",
    )(x, o_a, o_b, y, y, mod, w_dn, w_sw, w_o)


_BD_TAIL_BLOCKS = 3
TM_PROMPT = 1024
TF = 256
TN = 1024
TM_MIX = 256
SAMPLE_BATCH_PER_STEP = 4


def _reorder_w_in(w_in, d):
    sizes = (DN_CONV_CH, DN_V, DN_HEADS, DN_HEADS, SW_Q, SW_KV, SW_KV, d, d)
    o = np.concatenate([[0], np.cumsum(sizes)])
    seg = lambda i: w_in[:, :, o[i]:o[i + 1]]
    pad = jnp.zeros(w_in.shape[:2] + (BD_WIDTH - 2 * DN_HEADS + _BD_TAIL_BLOCKS * LANES,), w_in.dtype)
    parts = [seg(0), seg(1), seg(7), seg(8), seg(4), seg(5), seg(6), seg(2), seg(3), pad]
    return jnp.concatenate(parts, axis=-1).astype(BF16)


def kernel(x_prompt, x_sample, c_prompt, c_sample, state_delta, state_conv, cache_swa_k, cache_swa_v, rel_bias, w_ada, b_ada, norm_ffn1, w_ffn1_in, w_ffn1_out, norm_mix, w_in, dn_conv_w, dn_a_log, dn_dt_bias, dn_out_norm, w_dn_out, sw_q_norm, sw_k_norm, sw_sinks, w_sw_out, w_o, norm_ffn2, w_ffn2_in, w_ffn2_out):
    bp, seq, d = x_prompt.shape
    bd, dseq, _ = x_sample.shape
    depth = w_ada.shape[0]
    wc = cache_swa_k.shape[2]
    col_q = COL_GA + 2 * d

    n_rows = bp + bd
    pad_rows = -n_rows % SUBLANES
    c_all = jnp.concatenate([c_prompt, c_sample, jnp.zeros((pad_rows, d), F32)], axis=0)
    mod_all = _ada(c_all, w_ada, b_ada)
    mod_p = mod_all[:, :bp].reshape(depth, bp, 1, N_MOD * d)
    mod_s = jnp.repeat(mod_all[:, bp:n_rows], dseq, axis=1).reshape(depth, 1, bd * dseq, N_MOD * d)

    qi = np.arange(SW_BLOCK)[:, None]
    kj = np.arange(2 * SW_BLOCK)[None, :]
    bias_p = _bias_grid(rel_bias, _bucket_map(qi + SW_BLOCK - kj))
    ti = np.arange(dseq)[:, None]
    sj = np.arange(wc + dseq)[None, :]
    dist_s = ti + wc - sj
    bias_sb = _bias_grid(rel_bias, _bucket_map(dist_s[:, :wc]))
    bias_sn = _bias_grid(rel_bias, _bucket_map(dist_s[:, wc:]))

    w_in_r = _reorder_w_in(w_in, d)
    w_dn16 = w_dn_out.astype(BF16)
    w_sw16 = w_sw_out.astype(BF16)
    w_o16 = w_o.astype(BF16)
    r3 = lambda a: a.reshape(depth, 1, a.shape[-1])
    lane_row = lambda a: jnp.pad(a, ((0, 0), (DN_HEADS, LANES - 2 * DN_HEADS))).reshape(depth, 1, LANES)
    alog_row = lane_row(dn_a_log)
    dt_row = lane_row(dn_dt_bias)
    sinks = sw_sinks.reshape(depth, SW_KV_HEADS, SW_GROUP, 1)
    sink_p = jnp.repeat(sinks, SW_BLOCK, axis=2)
    sink_s = jnp.repeat(sinks, dseq, axis=2)
    kbuf = cache_swa_k.reshape(depth, bd, wc, SW_KV)
    vbuf = cache_swa_v.reshape(depth, bd, wc, SW_KV)
    g_ffn1, g_mix, g_ffn2, g_on = r3(norm_ffn1), r3(norm_mix), r3(norm_ffn2), r3(dn_out_norm)
    g_qn, g_kn = r3(sw_q_norm), r3(sw_k_norm)

    xp = x_prompt.reshape(bp * seq, d)
    xs = x_sample.reshape(bd * dseq, d)
    ms = bd * dseq
    outs = [[] for _ in range(8)]
    for l in range(depth):
        xp = _ffn(xp, mod_p, 0, l, g_ffn1, w_ffn1_in, w_ffn1_out, TM_PROMPT, TF)
        yp = _inproj(xp, mod_p, l, g_mix, w_in_r, TM_PROMPT, TN)
        oa, s_p, conv_p = _gdn(yp, bp, l, dn_conv_w, alog_row, dt_row, g_on, None, None, DN_CHUNK, BF16)
        ob, k_p, v_p = _swa_prompt(yp, bp, l, g_qn, g_kn, sink_p, bias_p, col_q, BF16)
        xp = _mixout(xp, oa, ob, yp, mod_p, l, w_dn16, w_sw16, w_o16, TM_MIX)
        xp = _ffn(xp, mod_p, 6, l, g_ffn2, w_ffn2_in, w_ffn2_out, TM_PROMPT, TF)
        xs = _ffn(xs, mod_s, 0, l, g_ffn1, w_ffn1_in, w_ffn1_out, ms, TF)
        ys = _inproj(xs, mod_s, l, g_mix, w_in_r, ms, TN)
        oa, s_s, conv_s = _gdn(ys, bd, l, dn_conv_w, alog_row, dt_row, g_on, state_conv, state_delta, dseq, F32)
        ob, k_s, v_s = _swa_sample(ys.reshape(bd, dseq, -1), l, kbuf, vbuf, g_qn, g_kn, sink_s, bias_sb, bias_sn,
                                   col_q, SAMPLE_BATCH_PER_STEP)
        xs = _mixout(xs, oa, ob.reshape(ms, SW_Q), ys, mod_s, l, w_dn16, w_sw16, w_o16, ms)
        xs = _ffn(xs, mod_s, 6, l, g_ffn2, w_ffn2_in, w_ffn2_out, ms, TF)
        for lst, val in zip(outs, (s_p, conv_p, k_p, v_p, s_s, conv_s, k_s, v_s)):
            lst.append(val)
    s_p, conv_p, k_p, v_p, s_s, conv_s, k_s, v_s = (jnp.stack(o) for o in outs)
    kv = lambda a: a.reshape(a.shape[:3] + (SW_KV_HEADS, SW_HD))
    return (xp.reshape(bp, seq, d), xs.reshape(bd, dseq, d), s_p, conv_p, kv(k_p), kv(v_p),
            s_s, conv_s, kv(k_s), kv(v_s))
```

```python
import functools
import math

import numpy as np
import jax
import jax.numpy as jnp
from jax import lax
from jax.experimental import pallas as pl
from jax.experimental.pallas import tpu as pltpu

F32 = jnp.float32
BF16 = jnp.bfloat16

NORM_EPS = 1e-6
NEG_INF = -1e30
N_MOD = 9

DN_HEADS = 8
DN_DK = 128
DN_DV = 128
DN_CONV = 4
DN_CHUNK = 64
DN_QK = DN_HEADS * DN_DK
DN_V = DN_HEADS * DN_DV
DN_CONV_CH = 2 * DN_QK + DN_V
DN_SCALE = DN_DK ** -0.5

SW_HEADS = 16
SW_KV_HEADS = 4
SW_HD = 64
SW_GROUP = SW_HEADS // SW_KV_HEADS
SW_SCALE = SW_HD ** -0.5
WINDOW = 128
SW_BLOCK = 128
SW_Q = SW_HEADS * SW_HD
SW_KV = SW_KV_HEADS * SW_HD

N_BUCKETS = 32
MAX_DISTANCE = 128

LANES = 128
SUBLANES = 8
VMEM_LIMIT_BYTES = 56 * 1024 * 1024

COL_CONV = 0
COL_Z = COL_CONV + DN_CONV_CH
COL_GA = COL_Z + DN_V
BD_WIDTH = LANES


def _sigmoid(x):
    return 1.0 / (1.0 + jnp.exp(-x))


def _silu(x):
    return x * _sigmoid(x)


def _softplus(x):
    return jnp.maximum(x, 0.0) + jnp.log1p(jnp.exp(-jnp.abs(x)))


def _dot(a, b, **kw):
    return jnp.dot(a, b, preferred_element_type=F32, **kw)


def _dot_nt(a, b):
    return lax.dot_general(a, b, (((1,), (1,)), ((), ())), preferred_element_type=F32)


def _dot_tn(a, b):
    return lax.dot_general(a, b, (((0,), (0,)), ((), ())), preferred_element_type=F32)


def _params(*semantics):
    return pltpu.CompilerParams(dimension_semantics=semantics, vmem_limit_bytes=VMEM_LIMIT_BYTES)


def _mod_norm(x, gain, sc, sh):
    y = x * lax.rsqrt(jnp.mean(x * x, axis=-1, keepdims=True) + NORM_EPS) * gain
    return y * (1.0 + sc) + sh


def _ada_kernel(c_ref, w_ref, b_ref, o_ref):
    a = _silu(c_ref[...]).astype(BF16)
    o_ref[...] = _dot(a, w_ref[...].astype(BF16)) + b_ref[...]


def _ada(c_all, w_ada, b_ada, tn=1024):
    depth, d, n = w_ada.shape
    rows = c_all.shape[0]
    return pl.pallas_call(
        _ada_kernel,
        grid=(depth, n // tn),
        in_specs=[
            pl.BlockSpec((rows, d), lambda l, j: (0, 0)),
            pl.BlockSpec((None, d, tn), lambda l, j: (l, 0, j)),
            pl.BlockSpec((None, 1, tn), lambda l, j: (l, 0, j)),
        ],
        out_specs=pl.BlockSpec((None, rows, tn), lambda l, j: (l, 0, j)),
        out_shape=jax.ShapeDtypeStruct((depth, rows, n), F32),
        compiler_params=_params("arbitrary", "arbitrary"),
        name="ada",
    )(c_all, w_ada, b_ada.reshape(depth, 1, n))


def _bucket_table():
    max_exact = N_BUCKETS // 2
    d = np.arange(WINDOW + 1)
    ratio = np.log(np.maximum(d, 1).astype(np.float32) / np.float32(max_exact)) / np.float32(math.log(MAX_DISTANCE / max_exact))
    large = max_exact + (ratio.astype(np.float32) * np.float32(N_BUCKETS - max_exact)).astype(np.int32)
    return np.where(d < max_exact, d, np.minimum(large, N_BUCKETS - 1)).astype(np.int32)


def _bucket_map(dist):
    table = _bucket_table()
    valid = (dist >= 0) & (dist <= WINDOW)
    return np.where(valid, table[np.clip(dist, 0, WINDOW)], -1).astype(np.int32)


def _bias_kernel(rb_ref, bm_ref, o_ref):
    bm = bm_ref[...]
    tq = bm.shape[0]
    for g in range(SW_KV_HEADS):
        for j in range(SW_GROUP):
            acc = jnp.full(bm.shape, NEG_INF, F32)
            for b in range(N_BUCKETS):
                acc = jnp.where(bm == b, rb_ref[b, g * SW_GROUP + j], acc)
            o_ref[g, j * tq:(j + 1) * tq, :] = acc


def _bias_grid(rel_bias, bmap):
    tq, tk = bmap.shape
    return pl.pallas_call(
        _bias_kernel,
        in_specs=[
            pl.BlockSpec(memory_space=pltpu.SMEM),
            pl.BlockSpec((tq, tk), lambda: (0, 0)),
        ],
        out_specs=pl.BlockSpec((SW_KV_HEADS, SW_GROUP * tq, tk), lambda: (0, 0, 0)),
        out_shape=jax.ShapeDtypeStruct((SW_KV_HEADS, SW_GROUP * tq, tk), F32),
        name="relbias",
    )(rel_bias, jnp.asarray(bmap))


def _ffn_kernel(x_ref, sh_ref, sc_ref, g_ref, gain_ref, wg_ref, wu_ref, wo_ref, o_ref, h_ref, *, nf):
    f = pl.program_id(1)

    @pl.when(f == 0)
    def _():
        h_ref[...] = _mod_norm(x_ref[...], gain_ref[...], sc_ref[...], sh_ref[...]).astype(BF16)
        o_ref[...] = jnp.zeros_like(o_ref)

    h = h_ref[...]
    gate = _dot(h, wg_ref[...].astype(BF16))
    up = _dot(h, wu_ref[...].astype(BF16))
    a = (_silu(gate) * up).astype(BF16)
    o_ref[...] += _dot(a, wo_ref[...].astype(BF16))

    @pl.when(f == nf - 1)
    def _():
        o_ref[...] = x_ref[...] + (0.5 * g_ref[...]) * o_ref[...]


def _mod_spec(mod, layer, seg, d, tiles_per_row):
    r = mod.shape[2]
    return pl.BlockSpec((None, None, r, d), lambda i, j: (layer, i // tiles_per_row, 0, seg))


def _ffn(x, mod, seg0, layer, gain, w_in, w_out, tm, tf):
    m, d = x.shape
    ff = w_out.shape[1]
    nf = ff // tf
    tiles_per_row = (m // mod.shape[1]) // tm
    return pl.pallas_call(
        functools.partial(_ffn_kernel, nf=nf),
        grid=(m // tm, nf),
        in_specs=[
            pl.BlockSpec((tm, d), lambda i, f: (i, 0), pipeline_mode=pl.Buffered(1)),
            _mod_spec(mod, layer, seg0, d, tiles_per_row),
            _mod_spec(mod, layer, seg0 + 1, d, tiles_per_row),
            _mod_spec(mod, layer, seg0 + 2, d, tiles_per_row),
            pl.BlockSpec((None, 1, d), lambda i, f: (layer, 0, 0)),
            pl.BlockSpec((None, d, tf), lambda i, f: (layer, 0, f)),
            pl.BlockSpec((None, d, tf), lambda i, f: (layer, 0, f + nf)),
            pl.BlockSpec((None, tf, d), lambda i, f: (layer, f, 0)),
        ],
        out_specs=pl.BlockSpec((tm, d), lambda i, f: (i, 0)),
        out_shape=jax.ShapeDtypeStruct((m, d), F32),
        scratch_shapes=[pltpu.VMEM((tm, d), BF16)],
        compiler_params=_params("arbitrary", "arbitrary"),
        name="ffn",
    )(x, mod, mod, mod, gain, w_in, w_in, w_out)


def _inproj_kernel(x_ref, sh_ref, sc_ref, gain_ref, w_ref, o_ref, h_ref):
    @pl.when(pl.program_id(1) == 0)
    def _():
        h_ref[...] = _mod_norm(x_ref[...], gain_ref[...], sc_ref[...], sh_ref[...]).astype(BF16)

    o_ref[...] = _dot(h_ref[...], w_ref[...])


def _inproj(x, mod, layer, gain, w_r, tm, tn):
    m, d = x.shape
    nr = w_r.shape[2]
    tiles_per_row = (m // mod.shape[1]) // tm
    return pl.pallas_call(
        _inproj_kernel,
        grid=(m // tm, nr // tn),
        in_specs=[
            pl.BlockSpec((tm, d), lambda i, j: (i, 0)),
            _mod_spec(mod, layer, 3, d, tiles_per_row),
            _mod_spec(mod, layer, 4, d, tiles_per_row),
            pl.BlockSpec((None, 1, d), lambda i, j: (layer, 0, 0)),
            pl.BlockSpec((None, d, tn), lambda i, j: (layer, 0, j)),
        ],
        out_specs=pl.BlockSpec((tm, tn), lambda i, j: (i, j)),
        out_shape=jax.ShapeDtypeStruct((m, nr), F32),
        scratch_shapes=[pltpu.VMEM((tm, d), BF16)],
        compiler_params=_params("arbitrary", "arbitrary"),
        name="inproj",
    )(x, mod, mod, gain, w_r)


def _unit_lower_inverses(lmats, ri, ci):
    c = lmats[0].shape[0]
    nb = c // SUBLANES
    n = len(lmats)
    eye = (ri == ci).astype(F32)
    lb = [[l[SUBLANES * i:SUBLANES * (i + 1), :] for i in range(nb)] for l in lmats]
    xb = [[eye[SUBLANES * i:SUBLANES * (i + 1), :] for i in range(nb)] for _ in lmats]
    for j in range(SUBLANES - 1):
        for h in range(n):
            for i in range(nb):
                col = SUBLANES * i + j
                xb[h][i] = xb[h][i] - lb[h][i][:, col:col + 1] * xb[h][i][j:j + 1, :]
    ts = [x[0] if nb == 1 else jnp.concatenate(x, axis=0) for x in xb]
    shift = int(math.log2(SUBLANES))
    while (1 << shift) < c:
        rb = jnp.right_shift(ri, shift)
        cb = jnp.right_shift(ci, shift)
        off = (jnp.right_shift(rb, 1) == jnp.right_shift(cb, 1)) & (rb == cb + 1)
        t16 = [t.astype(BF16) for t in ts]
        ps = [_dot(t16[h], jnp.where(off, lmats[h], 0.0).astype(BF16)).astype(BF16) for h in range(n)]
        xs = [_dot(ps[h], t16[h]) for h in range(n)]
        ts = [ts[h] - xs[h] for h in range(n)]
        shift += 1
    return ts


def _gdn_kernel(*refs, chunk, has_init):
    if has_init:
        (xc_ref, z_ref, bd_ref, cw_ref, alog_ref, dt_ref, on_ref, conv0_ref, s0_ref,
         o_ref, s_ref, conv_ref, xs_ref) = refs
    else:
        (xc_ref, z_ref, bd_ref, cw_ref, alog_ref, dt_ref, on_ref,
         o_ref, s_ref, conv_ref, xs_ref) = refs
    t = pl.program_id(1)
    nt = pl.num_programs(1)
    c = chunk
    hist = SUBLANES

    @pl.when(t == 0)
    def _():
        xs_ref[0:hist, :] = jnp.zeros((hist, DN_CONV_CH), F32)
        if has_init:
            xs_ref[hist - (DN_CONV - 1):hist, :] = conv0_ref[...]
            s_ref[...] = s0_ref[...]
        else:
            s_ref[...] = jnp.zeros_like(s_ref)

    xs_ref[hist:hist + c, :] = xc_ref[...]
    acc = xs_ref[hist:hist + c, :] * cw_ref[DN_CONV - 1:DN_CONV, :]
    for i in range(DN_CONV - 1):
        off = hist - (DN_CONV - 1) + i
        acc = acc + xs_ref[off:off + c, :] * cw_ref[i:i + 1, :]
    xc = _silu(acc)

    @pl.when(t == nt - 1)
    def _():
        conv_ref[...] = xs_ref[hist + c - (DN_CONV - 1):hist + c, :]

    xs_ref[0:hist, :] = xs_ref[c:c + hist, :]

    bd = bd_ref[...]
    beta_all = _sigmoid(bd)
    g_all = -jnp.exp(alog_ref[...]) * _softplus(bd + dt_ref[...])
    ri = lax.broadcasted_iota(jnp.int32, (c, c), 0)
    ci = lax.broadcasted_iota(jnp.int32, (c, c), 1)
    incl = ri >= ci
    strict = ri > ci
    gc_all = _dot(incl.astype(F32), g_all, precision=lax.Precision.HIGHEST)
    if c < LANES:
        gc_pad = jnp.concatenate([gc_all, jnp.zeros((LANES - c, LANES), F32)], axis=0)
    else:
        gc_pad = gc_all
    gc_t = gc_pad.T

    on = on_ref[...]
    heads = range(DN_HEADS)
    qs, ks, vs, betas, gcols, glasts, decays, egcs = [], [], [], [], [], [], [], []
    for h in heads:
        q = xc[:, DN_DK * h:DN_DK * (h + 1)]
        k = xc[:, DN_QK + DN_DK * h:DN_QK + DN_DK * (h + 1)]
        qs.append(q * lax.rsqrt(jnp.sum(q * q, axis=-1, keepdims=True) + NORM_EPS) * DN_SCALE)
        ks.append(k * lax.rsqrt(jnp.sum(k * k, axis=-1, keepdims=True) + NORM_EPS))
        vs.append(xc[:, 2 * DN_QK + DN_DV * h:2 * DN_QK + DN_DV * (h + 1)])
        betas.append(beta_all[:, h:h + 1])
        gcol = gc_all[:, DN_HEADS + h:DN_HEADS + h + 1]
        grow = gc_t[DN_HEADS + h:DN_HEADS + h + 1, 0:c]
        gcols.append(gcol)
        glasts.append(gc_all[c - 1:c, DN_HEADS + h:DN_HEADS + h + 1])
        decays.append(jnp.exp(jnp.where(incl, gcol - grow, NEG_INF)))
        egcs.append(jnp.exp(gcol))
    kbs = [ks[h] * betas[h] for h in heads]
    k16 = [ks[h].astype(BF16) for h in heads]
    kk = [_dot_nt(kbs[h].astype(BF16), k16[h]) for h in heads]
    qk = [_dot_nt(qs[h].astype(BF16), k16[h]) for h in heads]
    lmats = [jnp.where(strict, kk[h] * decays[h], 0.0) for h in heads]
    tinv = _unit_lower_inverses(lmats, ri, ci)
    rhs = [jnp.concatenate([vs[h] * betas[h], kbs[h] * egcs[h]], axis=-1).astype(BF16) for h in heads]
    sol = [_dot(tinv[h].astype(BF16), rhs[h]) for h in heads]
    s_old = [s_ref[h] for h in heads]
    s16 = [s.astype(BF16) for s in s_old]
    wq = [jnp.concatenate([sol[h][:, DN_DV:], qs[h] * egcs[h]], axis=0).astype(BF16) for h in heads]
    wqs = [_dot(wq[h], s16[h]) for h in heads]
    v16 = [(sol[h][:, :DN_DV] - wqs[h][:c, :]).astype(BF16) for h in heads]
    av = [_dot((qk[h] * decays[h]).astype(BF16), v16[h]) for h in heads]
    kdec = [(ks[h] * jnp.exp(glasts[h] - gcols[h])).astype(BF16) for h in heads]
    kv = [_dot_tn(kdec[h], v16[h]) for h in heads]
    for h in heads:
        s_ref[h] = s_old[h] * jnp.exp(glasts[h]) + kv[h]
        o = wqs[h][c:, :] + av[h]
        o = o * lax.rsqrt(jnp.mean(o * o, axis=-1, keepdims=True) + NORM_EPS) * on
        zh = z_ref[:, DN_DV * h:DN_DV * (h + 1)]
        o_ref[:, DN_DV * h:DN_DV * (h + 1)] = (o * _silu(zh)).astype(o_ref.dtype)


def _gdn(y, batch, layer, conv_w, alog_row, dt_row, out_norm, conv0, s0, chunk, out_dtype):
    m = y.shape[0]
    seq = m // batch
    nt = seq // chunk
    has_init = conv0 is not None
    row = lambda b, t: b * nt + t
    in_specs = [
        pl.BlockSpec((chunk, DN_CONV_CH), lambda b, t: (row(b, t), COL_CONV // DN_CONV_CH)),
        pl.BlockSpec((chunk, DN_V), lambda b, t: (row(b, t), COL_Z // DN_V)),
        pl.BlockSpec((chunk, BD_WIDTH), lambda b, t: (row(b, t), y.shape[1] // BD_WIDTH - 1 - _BD_TAIL_BLOCKS)),
        pl.BlockSpec((None, DN_CONV, DN_CONV_CH), lambda b, t: (layer, 0, 0)),
        pl.BlockSpec((None, 1, LANES), lambda b, t: (layer, 0, 0)),
        pl.BlockSpec((None, 1, LANES), lambda b, t: (layer, 0, 0)),
        pl.BlockSpec((None, 1, DN_DV), lambda b, t: (layer, 0, 0)),
    ]
    args = [y, y, y, conv_w, alog_row, dt_row, out_norm]
    if has_init:
        in_specs += [
            pl.BlockSpec((None, None, DN_CONV - 1, DN_CONV_CH), lambda b, t: (layer, b, 0, 0)),
            pl.BlockSpec((None, None, DN_HEADS, DN_DK, DN_DV), lambda b, t: (layer, b, 0, 0, 0)),
        ]
        args += [conv0, s0]
    return pl.pallas_call(
        functools.partial(_gdn_kernel, chunk=chunk, has_init=has_init),
        grid=(batch, nt),
        in_specs=in_specs,
        out_specs=[
            pl.BlockSpec((chunk, DN_V), lambda b, t: (row(b, t), 0)),
            pl.BlockSpec((None, DN_HEADS, DN_DK, DN_DV), lambda b, t: (b, 0, 0, 0)),
            pl.BlockSpec((None, DN_CONV - 1, DN_CONV_CH), lambda b, t: (b, 0, 0)),
        ],
        out_shape=[
            jax.ShapeDtypeStruct((m, DN_V), out_dtype),
            jax.ShapeDtypeStruct((batch, DN_HEADS, DN_DK, DN_DV), F32),
            jax.ShapeDtypeStruct((batch, DN_CONV - 1, DN_CONV_CH), F32),
        ],
        scratch_shapes=[pltpu.VMEM((chunk + SUBLANES, DN_CONV_CH), F32)],
        compiler_params=_params("arbitrary", "arbitrary"),
        name="gdn",
    )(*args)


def _seg_rms_scale(x, heads):
    w = x.shape[1]
    hd = w // heads
    shift = int(math.log2(hd))
    seg = (jnp.right_shift(lax.broadcasted_iota(jnp.int32, (w, LANES), 0), shift)
           == lax.broadcasted_iota(jnp.int32, (w, LANES), 1)).astype(BF16)
    seg_t = (lax.broadcasted_iota(jnp.int32, (LANES, w), 0)
             == jnp.right_shift(lax.broadcasted_iota(jnp.int32, (LANES, w), 1), shift)).astype(BF16)
    x2 = x * x
    x2_hi = x2.astype(BF16)
    x2_lo = (x2 - x2_hi.astype(F32)).astype(BF16)
    scale = lax.rsqrt((_dot(x2_hi, seg) + _dot(x2_lo, seg)) * (1.0 / hd) + NORM_EPS)
    hi = scale.astype(BF16)
    lo = (scale - hi.astype(F32)).astype(BF16)
    return _dot(hi, seg_t) + _dot(lo, seg_t)


def _same_shape_fold(xs, op):
    acc = {}
    for x in xs:
        acc[x.shape] = x if x.shape not in acc else op(acc[x.shape], x)
    return list(acc.values())


def _attn_groups(groups):
    tq = groups[0][0][0].shape[0]
    qg = [jnp.concatenate(g[0], axis=0).astype(BF16) for g in groups]
    raw = [[_dot_nt(q, k) for k in g[1]] for q, g in zip(qg, groups)]
    probs, sink_terms = [], []
    for r, g in zip(raw, groups):
        sink = g[4]
        ss = [s + b for s, b in zip(r, g[3])]
        m = sink
        for s in _same_shape_fold(ss, jnp.maximum):
            m = jnp.maximum(m, jnp.max(s, axis=-1, keepdims=True))
        probs.append([jnp.exp(s - m).astype(BF16) for s in ss])
        sink_terms.append(jnp.exp(sink - m))
    pv = [[_dot(p, v) for p, v in zip(ps, g[2])] for ps, g in zip(probs, groups)]
    rowsum = [[_dot(p, jnp.ones((p.shape[1], LANES), BF16)) for p in ps] for ps in probs]
    outs = []
    for parts, sums, es in zip(pv, rowsum, sink_terms):
        o = parts[0]
        for part in parts[1:]:
            o = o + part
        den = sums[0]
        for part in sums[1:]:
            den = den + part
        o = o * (1.0 / (den + es))[:, :o.shape[1]]
        outs += [o[tq * j:tq * (j + 1), :] for j in range(SW_GROUP)]
    return outs


def _head_cols(x, heads):
    return [x[:, SW_HD * h:SW_HD * (h + 1)] for h in heads]


def _swa_prompt_kernel(q_ref, kp_ref, kc_ref, vp_ref, vc_ref, kq_ref, kn_ref, sink_ref, bias_ref,
                       o_ref, kout_ref, vout_ref):
    n = pl.program_id(1)
    nb = pl.num_programs(1)
    q = q_ref[...]
    qn = q * _seg_rms_scale(q, SW_HEADS)
    kc = kc_ref[...]
    kc = kc * _seg_rms_scale(kc, SW_KV_HEADS) * kn_ref[...]
    kp = kp_ref[...]
    kp = kp * _seg_rms_scale(kp, SW_KV_HEADS) * kn_ref[...]
    kps = _head_cols(kp * kq_ref[...], range(SW_KV_HEADS))
    kcs = _head_cols(kc * kq_ref[...], range(SW_KV_HEADS))
    vc = vc_ref[...]
    vps = _head_cols(vp_ref[...], range(SW_KV_HEADS))
    vcs = _head_cols(vc, range(SW_KV_HEADS))
    first = n == 0
    groups = []
    for g in range(SW_KV_HEADS):
        bias = bias_ref[g]
        bias_prev = jnp.where(first, NEG_INF, bias[:, :SW_BLOCK])
        groups.append((
            _head_cols(qn, range(SW_GROUP * g, SW_GROUP * (g + 1))),
            [kps[g].astype(BF16), kcs[g].astype(BF16)],
            [vps[g].astype(BF16), vcs[g].astype(BF16)],
            [bias_prev, bias[:, SW_BLOCK:]],
            sink_ref[g]))
    o_ref[...] = jnp.concatenate(_attn_groups(groups), axis=-1).astype(o_ref.dtype)

    @pl.when(n == nb - 1)
    def _():
        kout_ref[...] = kc
        vout_ref[...] = vc


def _swa_prompt(y, batch, layer, q_norm, k_norm, sink_col, bias, col_q, out_dtype):
    m = y.shape[0]
    seq = m // batch
    nb = seq // SW_BLOCK
    cq = col_q // SW_Q
    ck = (col_q + SW_Q) // SW_KV
    cv = ck + 1
    cur = lambda b, n: b * nb + n
    prev = lambda b, n: b * nb + jnp.maximum(n - 1, 0)
    return pl.pallas_call(
        _swa_prompt_kernel,
        grid=(batch, nb),
        in_specs=[
            pl.BlockSpec((SW_BLOCK, SW_Q), lambda b, n: (cur(b, n), cq)),
            pl.BlockSpec((SW_BLOCK, SW_KV), lambda b, n: (prev(b, n), ck)),
            pl.BlockSpec((SW_BLOCK, SW_KV), lambda b, n: (cur(b, n), ck)),
            pl.BlockSpec((SW_BLOCK, SW_KV), lambda b, n: (prev(b, n), cv)),
            pl.BlockSpec((SW_BLOCK, SW_KV), lambda b, n: (cur(b, n), cv)),
            pl.BlockSpec((None, 1, SW_KV), lambda b, n: (layer, 0, 0)),
            pl.BlockSpec((None, 1, SW_KV), lambda b, n: (layer, 0, 0)),
            pl.BlockSpec((None, SW_KV_HEADS, SW_GROUP * SW_BLOCK, 1), lambda b, n: (layer, 0, 0, 0)),
            pl.BlockSpec((SW_KV_HEADS, SW_GROUP * SW_BLOCK, 2 * SW_BLOCK), lambda b, n: (0, 0, 0)),
        ],
        out_specs=[
            pl.BlockSpec((SW_BLOCK, SW_Q), lambda b, n: (cur(b, n), 0)),
            pl.BlockSpec((None, WINDOW, SW_KV), lambda b, n: (b, 0, 0)),
            pl.BlockSpec((None, WINDOW, SW_KV), lambda b, n: (b, 0, 0)),
        ],
        out_shape=[
            jax.ShapeDtypeStruct((m, SW_Q), out_dtype),
            jax.ShapeDtypeStruct((batch, WINDOW, SW_KV), F32),
            jax.ShapeDtypeStruct((batch, WINDOW, SW_KV), F32),
        ],
        compiler_params=_params("arbitrary", "arbitrary"),
        name="swa_prompt",
    )(y, y, y, y, y, q_norm, k_norm, sink_col, bias)


def _swa_sample_kernel(q_ref, k_ref, v_ref, kbuf_ref, vbuf_ref, kq_ref, kn_ref, sink_ref, bb_ref, bn_ref,
                       o_ref, kout_ref, vout_ref, *, bs):
    t = q_ref.shape[1]
    wc = kbuf_ref.shape[1]
    kv_heads = range(SW_KV_HEADS)
    groups = []
    for i in range(bs):
        q = q_ref[i]
        qn = q * _seg_rms_scale(q, SW_HEADS)
        kn = k_ref[i]
        kn = kn * _seg_rms_scale(kn, SW_KV_HEADS) * kn_ref[...]
        v = v_ref[i]
        kbuf = kbuf_ref[i]
        vbuf = vbuf_ref[i]
        kbs = _head_cols(kbuf * kq_ref[...], kv_heads)
        kns = _head_cols(kn * kq_ref[...], kv_heads)
        vbs = _head_cols(vbuf, kv_heads)
        vns = _head_cols(v, kv_heads)
        for g in kv_heads:
            groups.append((
                _head_cols(qn, range(SW_GROUP * g, SW_GROUP * (g + 1))),
                [kbs[g].astype(BF16), kns[g].astype(BF16)],
                [vbs[g].astype(BF16), vns[g].astype(BF16)],
                [bb_ref[g], bn_ref[g]],
                sink_ref[g]))
        kout_ref[i, 0:wc - t, :] = kbuf[t:, :]
        kout_ref[i, wc - t:wc, :] = kn
        vout_ref[i, 0:wc - t, :] = vbuf[t:, :]
        vout_ref[i, wc - t:wc, :] = v
    outs = _attn_groups(groups)
    for i in range(bs):
        o_ref[i] = jnp.concatenate(outs[SW_HEADS * i:SW_HEADS * (i + 1)], axis=-1).astype(o_ref.dtype)


def _swa_sample(y3, layer, k_buf, v_buf, q_norm, k_norm, sink_col, bias_buf, bias_new, col_q, bs):
    batch, t, _ = y3.shape
    wc = k_buf.shape[2]
    cq = col_q // SW_Q
    ck = (col_q + SW_Q) // SW_KV
    cv = ck + 1
    return pl.pallas_call(
        functools.partial(_swa_sample_kernel, bs=bs),
        grid=(batch // bs,),
        in_specs=[
            pl.BlockSpec((bs, t, SW_Q), lambda b: (b, 0, cq)),
            pl.BlockSpec((bs, t, SW_KV), lambda b: (b, 0, ck)),
            pl.BlockSpec((bs, t, SW_KV), lambda b: (b, 0, cv)),
            pl.BlockSpec((None, bs, wc, SW_KV), lambda b: (layer, b, 0, 0)),
            pl.BlockSpec((None, bs, wc, SW_KV), lambda b: (layer, b, 0, 0)),
            pl.BlockSpec((None, 1, SW_KV), lambda b: (layer, 0, 0)),
            pl.BlockSpec((None, 1, SW_KV), lambda b: (layer, 0, 0)),
            pl.BlockSpec((None, SW_KV_HEADS, SW_GROUP * t, 1), lambda b: (layer, 0, 0, 0)),
            pl.BlockSpec((SW_KV_HEADS, SW_GROUP * t, wc), lambda b: (0, 0, 0)),
            pl.BlockSpec((SW_KV_HEADS, SW_GROUP * t, t), lambda b: (0, 0, 0)),
        ],
        out_specs=[
            pl.BlockSpec((bs, t, SW_Q), lambda b: (b, 0, 0)),
            pl.BlockSpec((bs, wc, SW_KV), lambda b: (b, 0, 0)),
            pl.BlockSpec((bs, wc, SW_KV), lambda b: (b, 0, 0)),
        ],
        out_shape=[
            jax.ShapeDtypeStruct((batch, t, SW_Q), F32),
            jax.ShapeDtypeStruct((batch, wc, SW_KV), F32),
            jax.ShapeDtypeStruct((batch, wc, SW_KV), F32),
        ],
        compiler_params=_params("arbitrary"),
        name="swa_sample",
    )(y3, y3, y3, k_buf, v_buf, q_norm, k_norm, sink_col, bias_buf, bias_new)


def _mixout_kernel(x_ref, oa_ref, ob_ref, ga_ref, gb_ref, g_ref, wdn_ref, wsw_ref, wo_ref, o_ref):
    ya = _dot(oa_ref[...].astype(BF16), wdn_ref[...])
    yb = _dot(ob_ref[...].astype(BF16), wsw_ref[...])
    merged = _sigmoid(ga_ref[...]) * ya + _sigmoid(gb_ref[...]) * yb
    o_ref[...] = x_ref[...] + g_ref[...] * _dot(merged.astype(BF16), wo_ref[...])


def _mixout(x, o_a, o_b, y, mod, layer, w_dn, w_sw, w_o, tm):
    m, d = x.shape
    tiles_per_row = (m // mod.shape[1]) // tm
    cga = COL_GA // d
    const = dict(pipeline_mode=pl.Buffered(1))
    return pl.pallas_call(
        _mixout_kernel,
        grid=(m // tm,),
        in_specs=[
            pl.BlockSpec((tm, d), lambda i: (i, 0)),
            pl.BlockSpec((tm, DN_V), lambda i: (i, 0)),
            pl.BlockSpec((tm, SW_Q), lambda i: (i, 0)),
            pl.BlockSpec((tm, d), lambda i: (i, cga)),
            pl.BlockSpec((tm, d), lambda i: (i, cga + 1)),
            pl.BlockSpec((None, None, mod.shape[2], d), lambda i: (layer, i // tiles_per_row, 0, 5)),
            pl.BlockSpec((None, DN_V, d), lambda i: (layer, 0, 0), **const),
            pl.BlockSpec((None, SW_Q, d), lambda i: (layer, 0, 0), **const),
            pl.BlockSpec((None, d, d), lambda i: (layer, 0, 0), **const),
        ],
        out_specs=pl.BlockSpec((tm, d), lambda i: (i, 0)),
        out_shape=jax.ShapeDtypeStruct((m, d), F32),
        compiler_params=_params("arbitrary"),
        name="mixout",
    )(x, o_a, o_b, y, y, mod, w_dn, w_sw, w_o)


_BD_TAIL_BLOCKS = 3
TM_PROMPT = 1024
TF = 256
TN = 1024
TM_MIX = 256
SAMPLE_BATCH_PER_STEP = 4


def _reorder_w_in(w_in, d):
    sizes = (DN_CONV_CH, DN_V, DN_HEADS, DN_HEADS, SW_Q, SW_KV, SW_KV, d, d)
    o = np.concatenate([[0], np.cumsum(sizes)])
    seg = lambda i: w_in[:, :, o[i]:o[i + 1]]
    pad = jnp.zeros(w_in.shape[:2] + (BD_WIDTH - 2 * DN_HEADS + _BD_TAIL_BLOCKS * LANES,), w_in.dtype)
    parts = [seg(0), seg(1), seg(7), seg(8), seg(4), seg(5), seg(6), seg(2), seg(3), pad]
    return jnp.concatenate(parts, axis=-1).astype(BF16)


def kernel(x_prompt, x_sample, c_prompt, c_sample, state_delta, state_conv, cache_swa_k, cache_swa_v, rel_bias, w_ada, b_ada, norm_ffn1, w_ffn1_in, w_ffn1_out, norm_mix, w_in, dn_conv_w, dn_a_log, dn_dt_bias, dn_out_norm, w_dn_out, sw_q_norm, sw_k_norm, sw_sinks, w_sw_out, w_o, norm_ffn2, w_ffn2_in, w_ffn2_out):
    bp, seq, d = x_prompt.shape
    bd, dseq, _ = x_sample.shape
    depth = w_ada.shape[0]
    wc = cache_swa_k.shape[2]
    col_q = COL_GA + 2 * d

    n_rows = bp + bd
    pad_rows = -n_rows % SUBLANES
    c_all = jnp.concatenate([c_prompt, c_sample, jnp.zeros((pad_rows, d), F32)], axis=0)
    mod_all = _ada(c_all, w_ada, b_ada)
    mod_p = mod_all[:, :bp].reshape(depth, bp, 1, N_MOD * d)
    mod_s = jnp.repeat(mod_all[:, bp:n_rows], dseq, axis=1).reshape(depth, 1, bd * dseq, N_MOD * d)

    qi = np.arange(SW_BLOCK)[:, None]
    kj = np.arange(2 * SW_BLOCK)[None, :]
    bias_p = _bias_grid(rel_bias, _bucket_map(qi + SW_BLOCK - kj))
    ti = np.arange(dseq)[:, None]
    sj = np.arange(wc + dseq)[None, :]
    dist_s = ti + wc - sj
    bias_sb = _bias_grid(rel_bias, _bucket_map(dist_s[:, :wc]))
    bias_sn = _bias_grid(rel_bias, _bucket_map(dist_s[:, wc:]))

    w_in_r = _reorder_w_in(w_in, d)
    w_dn16 = w_dn_out.astype(BF16)
    w_sw16 = w_sw_out.astype(BF16)
    w_o16 = w_o.astype(BF16)
    r3 = lambda a: a.reshape(depth, 1, a.shape[-1])
    lane_row = lambda a: jnp.pad(a, ((0, 0), (DN_HEADS, LANES - 2 * DN_HEADS))).reshape(depth, 1, LANES)
    alog_row = lane_row(dn_a_log)
    dt_row = lane_row(dn_dt_bias)
    sinks = sw_sinks.reshape(depth, SW_KV_HEADS, SW_GROUP, 1)
    sink_p = jnp.repeat(sinks, SW_BLOCK, axis=2)
    sink_s = jnp.repeat(sinks, dseq, axis=2)
    kbuf = cache_swa_k.reshape(depth, bd, wc, SW_KV)
    vbuf = cache_swa_v.reshape(depth, bd, wc, SW_KV)
    g_ffn1, g_mix, g_ffn2, g_on = r3(norm_ffn1), r3(norm_mix), r3(norm_ffn2), r3(dn_out_norm)
    g_qn = r3(jnp.tile(sw_q_norm * SW_SCALE, (1, SW_KV_HEADS)))
    g_kn = r3(jnp.tile(sw_k_norm, (1, SW_KV_HEADS)))

    xp = x_prompt.reshape(bp * seq, d)
    xs = x_sample.reshape(bd * dseq, d)
    ms = bd * dseq
    outs = [[] for _ in range(8)]
    for l in range(depth):
        xp = _ffn(xp, mod_p, 0, l, g_ffn1, w_ffn1_in, w_ffn1_out, TM_PROMPT, TF)
        yp = _inproj(xp, mod_p, l, g_mix, w_in_r, TM_PROMPT, TN)
        oa, s_p, conv_p = _gdn(yp, bp, l, dn_conv_w, alog_row, dt_row, g_on, None, None, DN_CHUNK, BF16)
        ob, k_p, v_p = _swa_prompt(yp, bp, l, g_qn, g_kn, sink_p, bias_p, col_q, BF16)
        xp = _mixout(xp, oa, ob, yp, mod_p, l, w_dn16, w_sw16, w_o16, TM_MIX)
        xp = _ffn(xp, mod_p, 6, l, g_ffn2, w_ffn2_in, w_ffn2_out, TM_PROMPT, TF)
        xs = _ffn(xs, mod_s, 0, l, g_ffn1, w_ffn1_in, w_ffn1_out, ms, TF)
        ys = _inproj(xs, mod_s, l, g_mix, w_in_r, ms, TN)
        oa, s_s, conv_s = _gdn(ys, bd, l, dn_conv_w, alog_row, dt_row, g_on, state_conv, state_delta, dseq, F32)
        ob, k_s, v_s = _swa_sample(ys.reshape(bd, dseq, -1), l, kbuf, vbuf, g_qn, g_kn, sink_s, bias_sb, bias_sn,
                                   col_q, SAMPLE_BATCH_PER_STEP)
        xs = _mixout(xs, oa, ob.reshape(ms, SW_Q), ys, mod_s, l, w_dn16, w_sw16, w_o16, ms)
        xs = _ffn(xs, mod_s, 6, l, g_ffn2, w_ffn2_in, w_ffn2_out, ms, TF)
        for lst, val in zip(outs, (s_p, conv_p, k_p, v_p, s_s, conv_s, k_s, v_s)):
            lst.append(val)
    s_p, conv_p, k_p, v_p, s_s, conv_s, k_s, v_s = (jnp.stack(o) for o in outs)
    kv = lambda a: a.reshape(a.shape[:3] + (SW_KV_HEADS, SW_HD))
    return (xp.reshape(bp, seq, d), xs.reshape(bd, dseq, d), s_p, conv_p, kv(k_p), kv(v_p),
            s_s, conv_s, kv(k_s), kv(v_s))
```

```python
import functools
import math

import numpy as np
import jax
import jax.numpy as jnp
from jax import lax
from jax.experimental import pallas as pl
from jax.experimental.pallas import tpu as pltpu

F32 = jnp.float32
BF16 = jnp.bfloat16

NORM_EPS = 1e-6
NEG_INF = -1e30
N_MOD = 9

DN_HEADS = 8
DN_DK = 128
DN_DV = 128
DN_CONV = 4
DN_CHUNK = 64
DN_QK = DN_HEADS * DN_DK
DN_V = DN_HEADS * DN_DV
DN_CONV_CH = 2 * DN_QK + DN_V
DN_SCALE = DN_DK ** -0.5

SW_HEADS = 16
SW_KV_HEADS = 4
SW_HD = 64
SW_GROUP = SW_HEADS // SW_KV_HEADS
SW_SCALE = SW_HD ** -0.5
WINDOW = 128
SW_BLOCK = 128
SW_Q = SW_HEADS * SW_HD
SW_KV = SW_KV_HEADS * SW_HD

N_BUCKETS = 32
MAX_DISTANCE = 128

LANES = 128
SUBLANES = 8
VMEM_LIMIT_BYTES = 56 * 1024 * 1024

COL_CONV = 0
COL_Z = COL_CONV + DN_CONV_CH
COL_GA = COL_Z + DN_V
BD_WIDTH = LANES


def _sigmoid(x):
    return 1.0 / (1.0 + jnp.exp(-x))


def _silu(x):
    return x * _sigmoid(x)


def _softplus(x):
    return jnp.maximum(x, 0.0) + jnp.log1p(jnp.exp(-jnp.abs(x)))


def _dot(a, b, **kw):
    return jnp.dot(a, b, preferred_element_type=F32, **kw)


def _dot_nt(a, b):
    return lax.dot_general(a, b, (((1,), (1,)), ((), ())), preferred_element_type=F32)


def _dot_tn(a, b):
    return lax.dot_general(a, b, (((0,), (0,)), ((), ())), preferred_element_type=F32)


def _params(*semantics):
    return pltpu.CompilerParams(dimension_semantics=semantics, vmem_limit_bytes=VMEM_LIMIT_BYTES)


def _mod_norm(x, gain, sc, sh):
    y = x * lax.rsqrt(jnp.mean(x * x, axis=-1, keepdims=True) + NORM_EPS) * gain
    return y * (1.0 + sc) + sh


def _ada_kernel(c_ref, w_ref, b_ref, o_ref):
    a = _silu(c_ref[...]).astype(BF16)
    o_ref[...] = _dot(a, w_ref[...].astype(BF16)) + b_ref[...]


def _ada(c_all, w_ada, b_ada, tn=1024):
    depth, d, n = w_ada.shape
    rows = c_all.shape[0]
    return pl.pallas_call(
        _ada_kernel,
        grid=(depth, n // tn),
        in_specs=[
            pl.BlockSpec((rows, d), lambda l, j: (0, 0)),
            pl.BlockSpec((None, d, tn), lambda l, j: (l, 0, j)),
            pl.BlockSpec((None, 1, tn), lambda l, j: (l, 0, j)),
        ],
        out_specs=pl.BlockSpec((None, rows, tn), lambda l, j: (l, 0, j)),
        out_shape=jax.ShapeDtypeStruct((depth, rows, n), F32),
        compiler_params=_params("arbitrary", "arbitrary"),
        name="ada",
    )(c_all, w_ada, b_ada.reshape(depth, 1, n))


def _bucket_table():
    max_exact = N_BUCKETS // 2
    d = np.arange(WINDOW + 1)
    ratio = np.log(np.maximum(d, 1).astype(np.float32) / np.float32(max_exact)) / np.float32(math.log(MAX_DISTANCE / max_exact))
    large = max_exact + (ratio.astype(np.float32) * np.float32(N_BUCKETS - max_exact)).astype(np.int32)
    return np.where(d < max_exact, d, np.minimum(large, N_BUCKETS - 1)).astype(np.int32)


def _bucket_map(dist):
    table = _bucket_table()
    valid = (dist >= 0) & (dist <= WINDOW)
    return np.where(valid, table[np.clip(dist, 0, WINDOW)], -1).astype(np.int32)


def _bias_kernel(rb_ref, bm_ref, o_ref):
    bm = bm_ref[...]
    tq = bm.shape[0]
    for g in range(SW_KV_HEADS):
        for j in range(SW_GROUP):
            acc = jnp.full(bm.shape, NEG_INF, F32)
            for b in range(N_BUCKETS):
                acc = jnp.where(bm == b, rb_ref[b, g * SW_GROUP + j], acc)
            o_ref[g, j * tq:(j + 1) * tq, :] = acc


def _bias_grid(rel_bias, bmap):
    tq, tk = bmap.shape
    return pl.pallas_call(
        _bias_kernel,
        in_specs=[
            pl.BlockSpec(memory_space=pltpu.SMEM),
            pl.BlockSpec((tq, tk), lambda: (0, 0)),
        ],
        out_specs=pl.BlockSpec((SW_KV_HEADS, SW_GROUP * tq, tk), lambda: (0, 0, 0)),
        out_shape=jax.ShapeDtypeStruct((SW_KV_HEADS, SW_GROUP * tq, tk), F32),
        name="relbias",
    )(rel_bias, jnp.asarray(bmap))


def _ffn_kernel(x_ref, sh_ref, sc_ref, g_ref, gain_ref, wg_ref, wu_ref, wo_ref, o_ref, h_ref, *, nf):
    f = pl.program_id(1)

    @pl.when(f == 0)
    def _():
        h_ref[...] = _mod_norm(x_ref[...], gain_ref[...], sc_ref[...], sh_ref[...]).astype(BF16)
        o_ref[...] = jnp.zeros_like(o_ref)

    h = h_ref[...]
    gate = _dot(h, wg_ref[...].astype(BF16))
    up = _dot(h, wu_ref[...].astype(BF16))
    a = (_silu(gate) * up).astype(BF16)
    o_ref[...] += _dot(a, wo_ref[...].astype(BF16))

    @pl.when(f == nf - 1)
    def _():
        o_ref[...] = x_ref[...] + (0.5 * g_ref[...]) * o_ref[...]


def _mod_spec(mod, layer, seg, d, tiles_per_row):
    r = mod.shape[2]
    return pl.BlockSpec((None, None, r, d), lambda i, j: (layer, i // tiles_per_row, 0, seg))


def _ffn(x, mod, seg0, layer, gain, w_in, w_out, tm, tf):
    m, d = x.shape
    ff = w_out.shape[1]
    nf = ff // tf
    tiles_per_row = (m // mod.shape[1]) // tm
    return pl.pallas_call(
        functools.partial(_ffn_kernel, nf=nf),
        grid=(m // tm, nf),
        in_specs=[
            pl.BlockSpec((tm, d), lambda i, f: (i, 0), pipeline_mode=pl.Buffered(1)),
            _mod_spec(mod, layer, seg0, d, tiles_per_row),
            _mod_spec(mod, layer, seg0 + 1, d, tiles_per_row),
            _mod_spec(mod, layer, seg0 + 2, d, tiles_per_row),
            pl.BlockSpec((None, 1, d), lambda i, f: (layer, 0, 0)),
            pl.BlockSpec((None, d, tf), lambda i, f: (layer, 0, f)),
            pl.BlockSpec((None, d, tf), lambda i, f: (layer, 0, f + nf)),
            pl.BlockSpec((None, tf, d), lambda i, f: (layer, f, 0)),
        ],
        out_specs=pl.BlockSpec((tm, d), lambda i, f: (i, 0)),
        out_shape=jax.ShapeDtypeStruct((m, d), F32),
        scratch_shapes=[pltpu.VMEM((tm, d), BF16)],
        compiler_params=_params("arbitrary", "arbitrary"),
        name="ffn",
    )(x, mod, mod, mod, gain, w_in, w_in, w_out)


def _inproj_kernel(x_ref, sh_ref, sc_ref, gain_ref, w_ref, o_ref, h_ref):
    @pl.when(pl.program_id(1) == 0)
    def _():
        h_ref[...] = _mod_norm(x_ref[...], gain_ref[...], sc_ref[...], sh_ref[...]).astype(BF16)

    o_ref[...] = _dot(h_ref[...], w_ref[...])


def _inproj(x, mod, layer, gain, w_r, tm, tn):
    m, d = x.shape
    nr = w_r.shape[2]
    tiles_per_row = (m // mod.shape[1]) // tm
    return pl.pallas_call(
        _inproj_kernel,
        grid=(m // tm, nr // tn),
        in_specs=[
            pl.BlockSpec((tm, d), lambda i, j: (i, 0), pipeline_mode=pl.Buffered(1)),
            _mod_spec(mod, layer, 3, d, tiles_per_row),
            _mod_spec(mod, layer, 4, d, tiles_per_row),
            pl.BlockSpec((None, 1, d), lambda i, j: (layer, 0, 0)),
            pl.BlockSpec((None, d, tn), lambda i, j: (layer, 0, j)),
        ],
        out_specs=pl.BlockSpec((tm, tn), lambda i, j: (i, j)),
        out_shape=jax.ShapeDtypeStruct((m, nr), F32),
        scratch_shapes=[pltpu.VMEM((tm, d), BF16)],
        compiler_params=_params("arbitrary", "arbitrary"),
        name="inproj",
    )(x, mod, mod, gain, w_r)


def _unit_lower_inverses(lmats, ri, ci):
    c = lmats[0].shape[0]
    nb = c // SUBLANES
    n = len(lmats)
    eye = (ri == ci).astype(F32)
    lb = [[l[SUBLANES * i:SUBLANES * (i + 1), :] for i in range(nb)] for l in lmats]
    xb = [[eye[SUBLANES * i:SUBLANES * (i + 1), :] for i in range(nb)] for _ in lmats]
    for j in range(SUBLANES - 1):
        for h in range(n):
            for i in range(nb):
                col = SUBLANES * i + j
                xb[h][i] = xb[h][i] - lb[h][i][:, col:col + 1] * xb[h][i][j:j + 1, :]
    ts = [x[0] if nb == 1 else jnp.concatenate(x, axis=0) for x in xb]
    shift = int(math.log2(SUBLANES))
    while (1 << shift) < c:
        rb = jnp.right_shift(ri, shift)
        cb = jnp.right_shift(ci, shift)
        off = (jnp.right_shift(rb, 1) == jnp.right_shift(cb, 1)) & (rb == cb + 1)
        t16 = [t.astype(BF16) for t in ts]
        ps = [_dot(t16[h], jnp.where(off, lmats[h], 0.0).astype(BF16)).astype(BF16) for h in range(n)]
        xs = [_dot(ps[h], t16[h]) for h in range(n)]
        ts = [ts[h] - xs[h] for h in range(n)]
        shift += 1
    return ts


def _gdn_kernel(*refs, chunk, chunks_per_step, has_init):
    if has_init:
        (xc_ref, z_ref, bd_ref, cw_ref, alog_ref, dt_ref, on_ref, conv0_ref, s0_ref,
         o_ref, s_ref, conv_ref, xs_ref) = refs
    else:
        (xc_ref, z_ref, bd_ref, cw_ref, alog_ref, dt_ref, on_ref,
         o_ref, s_ref, conv_ref, xs_ref) = refs
    t = pl.program_id(1)
    nt = pl.num_programs(1)
    c = chunk
    tb = c * chunks_per_step
    hist = SUBLANES

    @pl.when(t == 0)
    def _():
        xs_ref[0:hist, :] = jnp.zeros((hist, DN_CONV_CH), F32)
        if has_init:
            xs_ref[hist - (DN_CONV - 1):hist, :] = conv0_ref[...]
            s_ref[...] = s0_ref[...]
        else:
            s_ref[...] = jnp.zeros_like(s_ref)

    xs_ref[hist:hist + tb, :] = xc_ref[...]
    acc = xs_ref[hist:hist + tb, :] * cw_ref[DN_CONV - 1:DN_CONV, :]
    for i in range(DN_CONV - 1):
        off = hist - (DN_CONV - 1) + i
        acc = acc + xs_ref[off:off + tb, :] * cw_ref[i:i + 1, :]
    xc_all = _silu(acc)

    @pl.when(t == nt - 1)
    def _():
        conv_ref[...] = xs_ref[hist + tb - (DN_CONV - 1):hist + tb, :]

    xs_ref[0:hist, :] = xs_ref[tb:tb + hist, :]

    bd = bd_ref[...]
    beta_blk = _sigmoid(bd)
    g_blk = -jnp.exp(alog_ref[...]) * _softplus(bd + dt_ref[...])
    ri = lax.broadcasted_iota(jnp.int32, (c, c), 0)
    ci = lax.broadcasted_iota(jnp.int32, (c, c), 1)
    incl = ri >= ci
    strict = ri > ci
    tri = incl.astype(F32)

    on = on_ref[...]
    heads = range(DN_HEADS)
    pairs = [(j, h) for j in range(chunks_per_step) for h in heads]
    qs, ks, vs, betas, gcols, glasts, decays, egcs = [], [], [], [], [], [], [], []
    for j in range(chunks_per_step):
        rows = slice(c * j, c * (j + 1))
        xc = xc_all[rows, :]
        beta_all = beta_blk[rows, :]
        gc_all = _dot(tri, g_blk[rows, :], precision=lax.Precision.HIGHEST)
        gc_pad = gc_all if c == LANES else jnp.concatenate([gc_all, jnp.zeros((LANES - c, LANES), F32)], axis=0)
        gc_t = gc_pad.T
        for h in heads:
            q = xc[:, DN_DK * h:DN_DK * (h + 1)]
            k = xc[:, DN_QK + DN_DK * h:DN_QK + DN_DK * (h + 1)]
            qs.append(q * lax.rsqrt(jnp.sum(q * q, axis=-1, keepdims=True) + NORM_EPS) * DN_SCALE)
            ks.append(k * lax.rsqrt(jnp.sum(k * k, axis=-1, keepdims=True) + NORM_EPS))
            vs.append(xc[:, 2 * DN_QK + DN_DV * h:2 * DN_QK + DN_DV * (h + 1)])
            betas.append(beta_all[:, h:h + 1])
            gcol = gc_all[:, DN_HEADS + h:DN_HEADS + h + 1]
            grow = gc_t[DN_HEADS + h:DN_HEADS + h + 1, 0:c]
            gcols.append(gcol)
            glasts.append(gc_all[c - 1:c, DN_HEADS + h:DN_HEADS + h + 1])
            decays.append(jnp.exp(jnp.where(incl, gcol - grow, NEG_INF)))
            egcs.append(jnp.exp(gcol))
    n = len(pairs)
    kbs = [ks[i] * betas[i] for i in range(n)]
    k16 = [ks[i].astype(BF16) for i in range(n)]
    kk = [_dot_nt(kbs[i].astype(BF16), k16[i]) for i in range(n)]
    qk = [_dot_nt(qs[i].astype(BF16), k16[i]) for i in range(n)]
    lmats = [jnp.where(strict, kk[i] * decays[i], 0.0) for i in range(n)]
    tinv = _unit_lower_inverses(lmats, ri, ci)
    rhs = [jnp.concatenate([vs[i] * betas[i], kbs[i] * egcs[i]], axis=-1).astype(BF16) for i in range(n)]
    sol = [_dot(tinv[i].astype(BF16), rhs[i]) for i in range(n)]
    wq = [jnp.concatenate([sol[i][:, DN_DV:], qs[i] * egcs[i]], axis=0).astype(BF16) for i in range(n)]
    a16 = [(qk[i] * decays[i]).astype(BF16) for i in range(n)]
    kdec = [(ks[i] * jnp.exp(glasts[i] - gcols[i])).astype(BF16) for i in range(n)]
    state = [s_ref[h] for h in heads]
    for j in range(chunks_per_step):
        idx = [j * DN_HEADS + h for h in heads]
        s16 = [s.astype(BF16) for s in state]
        wqs = [_dot(wq[i], s16[h]) for h, i in zip(heads, idx)]
        v16 = [(sol[i][:, :DN_DV] - wqs[h][:c, :]).astype(BF16) for h, i in zip(heads, idx)]
        av = [_dot(a16[i], v16[h]) for h, i in zip(heads, idx)]
        kv = [_dot_tn(kdec[i], v16[h]) for h, i in zip(heads, idx)]
        state = [state[h] * jnp.exp(glasts[i]) + kv[h] for h, i in zip(heads, idx)]
        for h in heads:
            o = wqs[h][c:, :] + av[h]
            o = o * lax.rsqrt(jnp.mean(o * o, axis=-1, keepdims=True) + NORM_EPS) * on
            zh = z_ref[c * j:c * (j + 1), DN_DV * h:DN_DV * (h + 1)]
            o_ref[c * j:c * (j + 1), DN_DV * h:DN_DV * (h + 1)] = (o * _silu(zh)).astype(o_ref.dtype)
    for h in heads:
        s_ref[h] = state[h]


def _gdn(y, batch, layer, conv_w, alog_row, dt_row, out_norm, conv0, s0, chunk, chunks_per_step, out_dtype):
    m = y.shape[0]
    seq = m // batch
    tb = chunk * chunks_per_step
    nt = seq // tb
    has_init = conv0 is not None
    row = lambda b, t: b * nt + t
    in_specs = [
        pl.BlockSpec((tb, DN_CONV_CH), lambda b, t: (row(b, t), COL_CONV // DN_CONV_CH)),
        pl.BlockSpec((tb, DN_V), lambda b, t: (row(b, t), COL_Z // DN_V)),
        pl.BlockSpec((tb, BD_WIDTH), lambda b, t: (row(b, t), y.shape[1] // BD_WIDTH - 1 - _BD_TAIL_BLOCKS)),
        pl.BlockSpec((None, DN_CONV, DN_CONV_CH), lambda b, t: (layer, 0, 0)),
        pl.BlockSpec((None, 1, LANES), lambda b, t: (layer, 0, 0)),
        pl.BlockSpec((None, 1, LANES), lambda b, t: (layer, 0, 0)),
        pl.BlockSpec((None, 1, DN_DV), lambda b, t: (layer, 0, 0)),
    ]
    args = [y, y, y, conv_w, alog_row, dt_row, out_norm]
    if has_init:
        in_specs += [
            pl.BlockSpec((None, None, DN_CONV - 1, DN_CONV_CH), lambda b, t: (layer, b, 0, 0)),
            pl.BlockSpec((None, None, DN_HEADS, DN_DK, DN_DV), lambda b, t: (layer, b, 0, 0, 0)),
        ]
        args += [conv0, s0]
    return pl.pallas_call(
        functools.partial(_gdn_kernel, chunk=chunk, chunks_per_step=chunks_per_step, has_init=has_init),
        grid=(batch, nt),
        in_specs=in_specs,
        out_specs=[
            pl.BlockSpec((tb, DN_V), lambda b, t: (row(b, t), 0)),
            pl.BlockSpec((None, DN_HEADS, DN_DK, DN_DV), lambda b, t: (b, 0, 0, 0)),
            pl.BlockSpec((None, DN_CONV - 1, DN_CONV_CH), lambda b, t: (b, 0, 0)),
        ],
        out_shape=[
            jax.ShapeDtypeStruct((m, DN_V), out_dtype),
            jax.ShapeDtypeStruct((batch, DN_HEADS, DN_DK, DN_DV), F32),
            jax.ShapeDtypeStruct((batch, DN_CONV - 1, DN_CONV_CH), F32),
        ],
        scratch_shapes=[pltpu.VMEM((tb + SUBLANES, DN_CONV_CH), F32)],
        compiler_params=_params("arbitrary", "arbitrary"),
        name="gdn",
    )(*args)


def _seg_rms_scale(x, heads):
    w = x.shape[1]
    hd = w // heads
    shift = int(math.log2(hd))
    seg = (jnp.right_shift(lax.broadcasted_iota(jnp.int32, (w, LANES), 0), shift)
           == lax.broadcasted_iota(jnp.int32, (w, LANES), 1)).astype(BF16)
    seg_t = (lax.broadcasted_iota(jnp.int32, (LANES, w), 0)
             == jnp.right_shift(lax.broadcasted_iota(jnp.int32, (LANES, w), 1), shift)).astype(BF16)
    x2 = x * x
    x2_hi = x2.astype(BF16)
    x2_lo = (x2 - x2_hi.astype(F32)).astype(BF16)
    scale = lax.rsqrt((_dot(x2_hi, seg) + _dot(x2_lo, seg)) * (1.0 / hd) + NORM_EPS)
    hi = scale.astype(BF16)
    lo = (scale - hi.astype(F32)).astype(BF16)
    return _dot(hi, seg_t) + _dot(lo, seg_t)


def _same_shape_fold(xs, op):
    acc = {}
    for x in xs:
        acc[x.shape] = x if x.shape not in acc else op(acc[x.shape], x)
    return list(acc.values())


def _attn_groups(groups):
    tq = groups[0][0][0].shape[0]
    qg = [jnp.concatenate(g[0], axis=0).astype(BF16) for g in groups]
    raw = [[_dot_nt(q, k) for k in g[1]] for q, g in zip(qg, groups)]
    probs, sink_terms = [], []
    for r, g in zip(raw, groups):
        sink = g[4]
        ss = [s + b for s, b in zip(r, g[3])]
        m = sink
        for s in _same_shape_fold(ss, jnp.maximum):
            m = jnp.maximum(m, jnp.max(s, axis=-1, keepdims=True))
        probs.append([jnp.exp(s - m).astype(BF16) for s in ss])
        sink_terms.append(jnp.exp(sink - m))
    pv = [[_dot(p, v) for p, v in zip(ps, g[2])] for ps, g in zip(probs, groups)]
    rowsum = [[_dot(p, jnp.ones((p.shape[1], LANES), BF16)) for p in ps] for ps in probs]
    outs = []
    for parts, sums, es in zip(pv, rowsum, sink_terms):
        o = parts[0]
        for part in parts[1:]:
            o = o + part
        den = sums[0]
        for part in sums[1:]:
            den = den + part
        o = o * (1.0 / (den + es))[:, :o.shape[1]]
        outs += [o[tq * j:tq * (j + 1), :] for j in range(SW_GROUP)]
    return outs


def _head_cols(x, heads):
    return [x[:, SW_HD * h:SW_HD * (h + 1)] for h in heads]


def _swa_prompt_kernel(q_ref, kp_ref, kc_ref, vp_ref, vc_ref, kq_ref, kn_ref, sink_ref, bias_ref,
                       o_ref, kout_ref, vout_ref):
    n = pl.program_id(1)
    nb = pl.num_programs(1)
    q = q_ref[...]
    qn = q * _seg_rms_scale(q, SW_HEADS)
    kc = kc_ref[...]
    kc = kc * _seg_rms_scale(kc, SW_KV_HEADS) * kn_ref[...]
    kp = kp_ref[...]
    kp = kp * _seg_rms_scale(kp, SW_KV_HEADS) * kn_ref[...]
    kps = _head_cols(kp * kq_ref[...], range(SW_KV_HEADS))
    kcs = _head_cols(kc * kq_ref[...], range(SW_KV_HEADS))
    vc = vc_ref[...]
    vps = _head_cols(vp_ref[...], range(SW_KV_HEADS))
    vcs = _head_cols(vc, range(SW_KV_HEADS))
    first = n == 0
    groups = []
    for g in range(SW_KV_HEADS):
        bias = bias_ref[g]
        bias_prev = jnp.where(first, NEG_INF, bias[:, :SW_BLOCK])
        groups.append((
            _head_cols(qn, range(SW_GROUP * g, SW_GROUP * (g + 1))),
            [kps[g].astype(BF16), kcs[g].astype(BF16)],
            [vps[g].astype(BF16), vcs[g].astype(BF16)],
            [bias_prev, bias[:, SW_BLOCK:]],
            sink_ref[g]))
    o_ref[...] = jnp.concatenate(_attn_groups(groups), axis=-1).astype(o_ref.dtype)

    @pl.when(n == nb - 1)
    def _():
        kout_ref[...] = kc
        vout_ref[...] = vc


def _swa_prompt(y, batch, layer, q_norm, k_norm, sink_col, bias, col_q, out_dtype):
    m = y.shape[0]
    seq = m // batch
    nb = seq // SW_BLOCK
    cq = col_q // SW_Q
    ck = (col_q + SW_Q) // SW_KV
    cv = ck + 1
    cur = lambda b, n: b * nb + n
    prev = lambda b, n: b * nb + jnp.maximum(n - 1, 0)
    return pl.pallas_call(
        _swa_prompt_kernel,
        grid=(batch, nb),
        in_specs=[
            pl.BlockSpec((SW_BLOCK, SW_Q), lambda b, n: (cur(b, n), cq)),
            pl.BlockSpec((SW_BLOCK, SW_KV), lambda b, n: (prev(b, n), ck)),
            pl.BlockSpec((SW_BLOCK, SW_KV), lambda b, n: (cur(b, n), ck)),
            pl.BlockSpec((SW_BLOCK, SW_KV), lambda b, n: (prev(b, n), cv)),
            pl.BlockSpec((SW_BLOCK, SW_KV), lambda b, n: (cur(b, n), cv)),
            pl.BlockSpec((None, 1, SW_KV), lambda b, n: (layer, 0, 0)),
            pl.BlockSpec((None, 1, SW_KV), lambda b, n: (layer, 0, 0)),
            pl.BlockSpec((None, SW_KV_HEADS, SW_GROUP * SW_BLOCK, 1), lambda b, n: (layer, 0, 0, 0)),
            pl.BlockSpec((SW_KV_HEADS, SW_GROUP * SW_BLOCK, 2 * SW_BLOCK), lambda b, n: (0, 0, 0)),
        ],
        out_specs=[
            pl.BlockSpec((SW_BLOCK, SW_Q), lambda b, n: (cur(b, n), 0)),
            pl.BlockSpec((None, WINDOW, SW_KV), lambda b, n: (b, 0, 0)),
            pl.BlockSpec((None, WINDOW, SW_KV), lambda b, n: (b, 0, 0)),
        ],
        out_shape=[
            jax.ShapeDtypeStruct((m, SW_Q), out_dtype),
            jax.ShapeDtypeStruct((batch, WINDOW, SW_KV), F32),
            jax.ShapeDtypeStruct((batch, WINDOW, SW_KV), F32),
        ],
        compiler_params=_params("arbitrary", "arbitrary"),
        name="swa_prompt",
    )(y, y, y, y, y, q_norm, k_norm, sink_col, bias)


def _swa_sample_kernel(q_ref, k_ref, v_ref, kbuf_ref, vbuf_ref, kq_ref, kn_ref, sink_ref, bb_ref, bn_ref,
                       o_ref, kout_ref, vout_ref, *, bs):
    t = q_ref.shape[1]
    wc = kbuf_ref.shape[1]
    kv_heads = range(SW_KV_HEADS)
    groups = []
    for i in range(bs):
        q = q_ref[i]
        qn = q * _seg_rms_scale(q, SW_HEADS)
        kn = k_ref[i]
        kn = kn * _seg_rms_scale(kn, SW_KV_HEADS) * kn_ref[...]
        v = v_ref[i]
        kbuf = kbuf_ref[i]
        vbuf = vbuf_ref[i]
        kbs = _head_cols(kbuf * kq_ref[...], kv_heads)
        kns = _head_cols(kn * kq_ref[...], kv_heads)
        vbs = _head_cols(vbuf, kv_heads)
        vns = _head_cols(v, kv_heads)
        for g in kv_heads:
            groups.append((
                _head_cols(qn, range(SW_GROUP * g, SW_GROUP * (g + 1))),
                [kbs[g].astype(BF16), kns[g].astype(BF16)],
                [vbs[g].astype(BF16), vns[g].astype(BF16)],
                [bb_ref[g], bn_ref[g]],
                sink_ref[g]))
        kout_ref[i, 0:wc - t, :] = kbuf[t:, :]
        kout_ref[i, wc - t:wc, :] = kn
        vout_ref[i, 0:wc - t, :] = vbuf[t:, :]
        vout_ref[i, wc - t:wc, :] = v
    outs = _attn_groups(groups)
    for i in range(bs):
        o_ref[i] = jnp.concatenate(outs[SW_HEADS * i:SW_HEADS * (i + 1)], axis=-1).astype(o_ref.dtype)


def _swa_sample(y3, layer, k_buf, v_buf, q_norm, k_norm, sink_col, bias_buf, bias_new, col_q, bs):
    batch, t, _ = y3.shape
    wc = k_buf.shape[2]
    cq = col_q // SW_Q
    ck = (col_q + SW_Q) // SW_KV
    cv = ck + 1
    return pl.pallas_call(
        functools.partial(_swa_sample_kernel, bs=bs),
        grid=(batch // bs,),
        in_specs=[
            pl.BlockSpec((bs, t, SW_Q), lambda b: (b, 0, cq)),
            pl.BlockSpec((bs, t, SW_KV), lambda b: (b, 0, ck)),
            pl.BlockSpec((bs, t, SW_KV), lambda b: (b, 0, cv)),
            pl.BlockSpec((None, bs, wc, SW_KV), lambda b: (layer, b, 0, 0)),
            pl.BlockSpec((None, bs, wc, SW_KV), lambda b: (layer, b, 0, 0)),
            pl.BlockSpec((None, 1, SW_KV), lambda b: (layer, 0, 0)),
            pl.BlockSpec((None, 1, SW_KV), lambda b: (layer, 0, 0)),
            pl.BlockSpec((None, SW_KV_HEADS, SW_GROUP * t, 1), lambda b: (layer, 0, 0, 0)),
            pl.BlockSpec((SW_KV_HEADS, SW_GROUP * t, wc), lambda b: (0, 0, 0)),
            pl.BlockSpec((SW_KV_HEADS, SW_GROUP * t, t), lambda b: (0, 0, 0)),
        ],
        out_specs=[
            pl.BlockSpec((bs, t, SW_Q), lambda b: (b, 0, 0)),
            pl.BlockSpec((bs, wc, SW_KV), lambda b: (b, 0, 0)),
            pl.BlockSpec((bs, wc, SW_KV), lambda b: (b, 0, 0)),
        ],
        out_shape=[
            jax.ShapeDtypeStruct((batch, t, SW_Q), F32),
            jax.ShapeDtypeStruct((batch, wc, SW_KV), F32),
            jax.ShapeDtypeStruct((batch, wc, SW_KV), F32),
        ],
        compiler_params=_params("arbitrary"),
        name="swa_sample",
    )(y3, y3, y3, k_buf, v_buf, q_norm, k_norm, sink_col, bias_buf, bias_new)


def _mixout_kernel(x_ref, oa_ref, ob_ref, ga_ref, gb_ref, g_ref, wdn_ref, wsw_ref, wo_ref, o_ref):
    ya = _dot(oa_ref[...].astype(BF16), wdn_ref[...])
    yb = _dot(ob_ref[...].astype(BF16), wsw_ref[...])
    merged = _sigmoid(ga_ref[...]) * ya + _sigmoid(gb_ref[...]) * yb
    o_ref[...] = x_ref[...] + g_ref[...] * _dot(merged.astype(BF16), wo_ref[...])


def _mixout(x, o_a, o_b, y, mod, layer, w_dn, w_sw, w_o, tm):
    m, d = x.shape
    tiles_per_row = (m // mod.shape[1]) // tm
    cga = COL_GA // d
    const = dict(pipeline_mode=pl.Buffered(1))
    return pl.pallas_call(
        _mixout_kernel,
        grid=(m // tm,),
        in_specs=[
            pl.BlockSpec((tm, d), lambda i: (i, 0)),
            pl.BlockSpec((tm, DN_V), lambda i: (i, 0)),
            pl.BlockSpec((tm, SW_Q), lambda i: (i, 0)),
            pl.BlockSpec((tm, d), lambda i: (i, cga)),
            pl.BlockSpec((tm, d), lambda i: (i, cga + 1)),
            pl.BlockSpec((None, None, mod.shape[2], d), lambda i: (layer, i // tiles_per_row, 0, 5)),
            pl.BlockSpec((None, DN_V, d), lambda i: (layer, 0, 0), **const),
            pl.BlockSpec((None, SW_Q, d), lambda i: (layer, 0, 0), **const),
            pl.BlockSpec((None, d, d), lambda i: (layer, 0, 0), **const),
        ],
        out_specs=pl.BlockSpec((tm, d), lambda i: (i, 0)),
        out_shape=jax.ShapeDtypeStruct((m, d), F32),
        compiler_params=_params("arbitrary"),
        name="mixout",
    )(x, o_a, o_b, y, y, mod, w_dn, w_sw, w_o)


_BD_TAIL_BLOCKS = 3
TM_PROMPT = 1024
TF = 256
TF_SAMPLE = 512
TN = 2048
TM_MIX = 256
GDN_CHUNKS_PER_STEP = 2
SAMPLE_BATCH_PER_STEP = 4


def _reorder_kernel(w_ref, o_ref, *, segments):
    col = 0
    for start, width in segments:
        o_ref[:, col:col + width] = w_ref[:, start:start + width].astype(o_ref.dtype)
        col += width
    o_ref[:, col:] = jnp.zeros((o_ref.shape[0], o_ref.shape[1] - col), o_ref.dtype)


def _reorder_w_in(w_in, d, tk=256):
    sizes = (DN_CONV_CH, DN_V, DN_HEADS, DN_HEADS, SW_Q, SW_KV, SW_KV, d, d)
    o = [int(v) for v in np.concatenate([[0], np.cumsum(sizes)])]
    segments = ((0, o[2]), (o[7], 2 * d), (o[4], o[7] - o[4]), (o[2], o[4] - o[2]))
    depth, rows, n_in = w_in.shape
    n_out = sum(w for _, w in segments) + BD_WIDTH - 2 * DN_HEADS + _BD_TAIL_BLOCKS * LANES
    return pl.pallas_call(
        functools.partial(_reorder_kernel, segments=segments),
        grid=(depth, rows // tk),
        in_specs=[pl.BlockSpec((None, tk, n_in), lambda l, i: (l, i, 0))],
        out_specs=pl.BlockSpec((None, tk, n_out), lambda l, i: (l, i, 0)),
        out_shape=jax.ShapeDtypeStruct((depth, rows, n_out), BF16),
        compiler_params=_params("arbitrary", "arbitrary"),
        name="reorder_w_in",
    )(w_in)


def kernel(x_prompt, x_sample, c_prompt, c_sample, state_delta, state_conv, cache_swa_k, cache_swa_v, rel_bias, w_ada, b_ada, norm_ffn1, w_ffn1_in, w_ffn1_out, norm_mix, w_in, dn_conv_w, dn_a_log, dn_dt_bias, dn_out_norm, w_dn_out, sw_q_norm, sw_k_norm, sw_sinks, w_sw_out, w_o, norm_ffn2, w_ffn2_in, w_ffn2_out):
    bp, seq, d = x_prompt.shape
    bd, dseq, _ = x_sample.shape
    depth = w_ada.shape[0]
    wc = cache_swa_k.shape[2]
    col_q = COL_GA + 2 * d

    n_rows = bp + bd
    pad_rows = -n_rows % SUBLANES
    c_all = jnp.concatenate([c_prompt, c_sample, jnp.zeros((pad_rows, d), F32)], axis=0)
    mod_all = _ada(c_all, w_ada, b_ada)
    mod_p = mod_all[:, :bp].reshape(depth, bp, 1, N_MOD * d)
    mod_s = jnp.repeat(mod_all[:, bp:n_rows], dseq, axis=1).reshape(depth, 1, bd * dseq, N_MOD * d)

    qi = np.arange(SW_BLOCK)[:, None]
    kj = np.arange(2 * SW_BLOCK)[None, :]
    bias_p = _bias_grid(rel_bias, _bucket_map(qi + SW_BLOCK - kj))
    ti = np.arange(dseq)[:, None]
    sj = np.arange(wc + dseq)[None, :]
    dist_s = ti + wc - sj
    bias_sb = _bias_grid(rel_bias, _bucket_map(dist_s[:, :wc]))
    bias_sn = _bias_grid(rel_bias, _bucket_map(dist_s[:, wc:]))

    w_in_r = _reorder_w_in(w_in, d)
    w_dn16 = w_dn_out.astype(BF16)
    w_sw16 = w_sw_out.astype(BF16)
    w_o16 = w_o.astype(BF16)
    r3 = lambda a: a.reshape(depth, 1, a.shape[-1])
    lane_row = lambda a: jnp.pad(a, ((0, 0), (DN_HEADS, LANES - 2 * DN_HEADS))).reshape(depth, 1, LANES)
    alog_row = lane_row(dn_a_log)
    dt_row = lane_row(dn_dt_bias)
    sinks = sw_sinks.reshape(depth, SW_KV_HEADS, SW_GROUP, 1)
    sink_p = jnp.repeat(sinks, SW_BLOCK, axis=2)
    sink_s = jnp.repeat(sinks, dseq, axis=2)
    kbuf = cache_swa_k.reshape(depth, bd, wc, SW_KV)
    vbuf = cache_swa_v.reshape(depth, bd, wc, SW_KV)
    g_ffn1, g_mix, g_ffn2, g_on = r3(norm_ffn1), r3(norm_mix), r3(norm_ffn2), r3(dn_out_norm)
    g_qn = r3(jnp.tile(sw_q_norm * SW_SCALE, (1, SW_KV_HEADS)))
    g_kn = r3(jnp.tile(sw_k_norm, (1, SW_KV_HEADS)))

    xp = x_prompt.reshape(bp * seq, d)
    xs = x_sample.reshape(bd * dseq, d)
    ms = bd * dseq
    outs = [[] for _ in range(8)]
    for l in range(depth):
        xp = _ffn(xp, mod_p, 0, l, g_ffn1, w_ffn1_in, w_ffn1_out, TM_PROMPT, TF)
        yp = _inproj(xp, mod_p, l, g_mix, w_in_r, TM_PROMPT, TN)
        oa, s_p, conv_p = _gdn(yp, bp, l, dn_conv_w, alog_row, dt_row, g_on, None, None, DN_CHUNK,
                               GDN_CHUNKS_PER_STEP, BF16)
        ob, k_p, v_p = _swa_prompt(yp, bp, l, g_qn, g_kn, sink_p, bias_p, col_q, BF16)
        xp = _mixout(xp, oa, ob, yp, mod_p, l, w_dn16, w_sw16, w_o16, TM_MIX)
        xp = _ffn(xp, mod_p, 6, l, g_ffn2, w_ffn2_in, w_ffn2_out, TM_PROMPT, TF)
        xs = _ffn(xs, mod_s, 0, l, g_ffn1, w_ffn1_in, w_ffn1_out, ms, TF_SAMPLE)
        ys = _inproj(xs, mod_s, l, g_mix, w_in_r, ms, TN)
        oa, s_s, conv_s = _gdn(ys, bd, l, dn_conv_w, alog_row, dt_row, g_on, state_conv, state_delta, dseq, 1, F32)
        ob, k_s, v_s = _swa_sample(ys.reshape(bd, dseq, -1), l, kbuf, vbuf, g_qn, g_kn, sink_s, bias_sb, bias_sn,
                                   col_q, SAMPLE_BATCH_PER_STEP)
        xs = _mixout(xs, oa, ob.reshape(ms, SW_Q), ys, mod_s, l, w_dn16, w_sw16, w_o16, ms)
        xs = _ffn(xs, mod_s, 6, l, g_ffn2, w_ffn2_in, w_ffn2_out, ms, TF_SAMPLE)
        for lst, val in zip(outs, (s_p, conv_p, k_p, v_p, s_s, conv_s, k_s, v_s)):
            lst.append(val)
    s_p, conv_p, k_p, v_p, s_s, conv_s, k_s, v_s = (jnp.stack(o) for o in outs)
    kv = lambda a: a.reshape(a.shape[:3] + (SW_KV_HEADS, SW_HD))
    return (xp.reshape(bp, seq, d), xs.reshape(bd, dseq, d), s_p, conv_p, kv(k_p), kv(v_p),
            s_s, conv_s, kv(k_s), kv(v_s))
```

```python
import functools
import math

import numpy as np
import jax
import jax.numpy as jnp
from jax import lax
from jax.experimental import pallas as pl
from jax.experimental.pallas import tpu as pltpu

F32 = jnp.float32
BF16 = jnp.bfloat16

NORM_EPS = 1e-6
NEG_INF = -1e30
N_MOD = 9

DN_HEADS = 8
DN_DK = 128
DN_DV = 128
DN_CONV = 4
DN_CHUNK = 64
DN_QK = DN_HEADS * DN_DK
DN_V = DN_HEADS * DN_DV
DN_CONV_CH = 2 * DN_QK + DN_V
DN_SCALE = DN_DK ** -0.5

SW_HEADS = 16
SW_KV_HEADS = 4
SW_HD = 64
SW_GROUP = SW_HEADS // SW_KV_HEADS
SW_SCALE = SW_HD ** -0.5
WINDOW = 128
SW_BLOCK = 128
SW_Q = SW_HEADS * SW_HD
SW_KV = SW_KV_HEADS * SW_HD

N_BUCKETS = 32
MAX_DISTANCE = 128

LANES = 128
SUBLANES = 8
VMEM_LIMIT_BYTES = 56 * 1024 * 1024
FFN_VMEM_LIMIT_BYTES = 60 * 1024 * 1024

COL_CONV = 0
COL_Z = COL_CONV + DN_CONV_CH
COL_GA = COL_Z + DN_V
BD_WIDTH = LANES


def _sigmoid(x):
    return 1.0 / (1.0 + jnp.exp(-x))


def _silu(x):
    return x * _sigmoid(x)


def _softplus(x):
    return jnp.maximum(x, 0.0) + jnp.log1p(jnp.exp(-jnp.abs(x)))


def _dot(a, b, **kw):
    return jnp.dot(a, b, preferred_element_type=F32, **kw)


def _dot_nt(a, b):
    return lax.dot_general(a, b, (((1,), (1,)), ((), ())), preferred_element_type=F32)


def _dot_tn(a, b):
    return lax.dot_general(a, b, (((0,), (0,)), ((), ())), preferred_element_type=F32)


def _params(*semantics, vmem_limit_bytes=VMEM_LIMIT_BYTES):
    return pltpu.CompilerParams(dimension_semantics=semantics, vmem_limit_bytes=vmem_limit_bytes)


def _mod_norm(x, gain, sc, sh):
    y = x * lax.rsqrt(jnp.mean(x * x, axis=-1, keepdims=True) + NORM_EPS) * gain
    return y * (1.0 + sc) + sh


def _ada_kernel(c_ref, w_ref, b_ref, o_ref):
    a = _silu(c_ref[...]).astype(BF16)
    o_ref[...] = _dot(a, w_ref[...].astype(BF16)) + b_ref[...]


def _ada(c_all, w_ada, b_ada, tn=1024):
    depth, d, n = w_ada.shape
    rows = c_all.shape[0]
    return pl.pallas_call(
        _ada_kernel,
        grid=(depth, n // tn),
        in_specs=[
            pl.BlockSpec((rows, d), lambda l, j: (0, 0)),
            pl.BlockSpec((None, d, tn), lambda l, j: (l, 0, j)),
            pl.BlockSpec((None, 1, tn), lambda l, j: (l, 0, j)),
        ],
        out_specs=pl.BlockSpec((None, rows, tn), lambda l, j: (l, 0, j)),
        out_shape=jax.ShapeDtypeStruct((depth, rows, n), F32),
        compiler_params=_params("arbitrary", "arbitrary"),
        name="ada",
    )(c_all, w_ada, b_ada.reshape(depth, 1, n))


def _bucket_table():
    max_exact = N_BUCKETS // 2
    d = np.arange(WINDOW + 1)
    ratio = np.log(np.maximum(d, 1).astype(np.float32) / np.float32(max_exact)) / np.float32(math.log(MAX_DISTANCE / max_exact))
    large = max_exact + (ratio.astype(np.float32) * np.float32(N_BUCKETS - max_exact)).astype(np.int32)
    return np.where(d < max_exact, d, np.minimum(large, N_BUCKETS - 1)).astype(np.int32)


def _bucket_map(dist):
    table = _bucket_table()
    valid = (dist >= 0) & (dist <= WINDOW)
    return np.where(valid, table[np.clip(dist, 0, WINDOW)], -1).astype(np.int32)


def _bias_kernel(rb_ref, bm_ref, o_ref):
    bm = bm_ref[...]
    tq = bm.shape[0]
    for g in range(SW_KV_HEADS):
        for j in range(SW_GROUP):
            acc = jnp.full(bm.shape, NEG_INF, F32)
            for b in range(N_BUCKETS):
                acc = jnp.where(bm == b, rb_ref[b, g * SW_GROUP + j], acc)
            o_ref[g, j * tq:(j + 1) * tq, :] = acc


def _bias_grid(rel_bias, bmap):
    tq, tk = bmap.shape
    return pl.pallas_call(
        _bias_kernel,
        in_specs=[
            pl.BlockSpec(memory_space=pltpu.SMEM),
            pl.BlockSpec((tq, tk), lambda: (0, 0)),
        ],
        out_specs=pl.BlockSpec((SW_KV_HEADS, SW_GROUP * tq, tk), lambda: (0, 0, 0)),
        out_shape=jax.ShapeDtypeStruct((SW_KV_HEADS, SW_GROUP * tq, tk), F32),
        name="relbias",
    )(rel_bias, jnp.asarray(bmap))


def _ffn_kernel(x_ref, sh_ref, sc_ref, g_ref, gain_ref, wg_ref, wu_ref, wo_ref, o_ref, h_ref, *, nf):
    f = pl.program_id(1)

    @pl.when(f == 0)
    def _():
        h_ref[...] = _mod_norm(x_ref[...], gain_ref[...], sc_ref[...], sh_ref[...]).astype(BF16)
        o_ref[...] = jnp.zeros_like(o_ref)

    h = h_ref[...]
    gate = _dot(h, wg_ref[...].astype(BF16))
    up = _dot(h, wu_ref[...].astype(BF16))
    a = (_silu(gate) * up).astype(BF16)
    o_ref[...] += _dot(a, wo_ref[...].astype(BF16))

    @pl.when(f == nf - 1)
    def _():
        o_ref[...] = x_ref[...] + (0.5 * g_ref[...]) * o_ref[...]


def _mod_spec(mod, layer, seg, d, tiles_per_row):
    r = mod.shape[2]
    return pl.BlockSpec((None, None, r, d), lambda i, j: (layer, i // tiles_per_row, 0, seg))


def _ffn(x, mod, seg0, layer, gain, w_in, w_out, tm, tf):
    m, d = x.shape
    ff = w_out.shape[1]
    nf = ff // tf
    tiles_per_row = (m // mod.shape[1]) // tm
    return pl.pallas_call(
        functools.partial(_ffn_kernel, nf=nf),
        grid=(m // tm, nf),
        in_specs=[
            pl.BlockSpec((tm, d), lambda i, f: (i, 0)),
            _mod_spec(mod, layer, seg0, d, tiles_per_row),
            _mod_spec(mod, layer, seg0 + 1, d, tiles_per_row),
            _mod_spec(mod, layer, seg0 + 2, d, tiles_per_row),
            pl.BlockSpec((None, 1, d), lambda i, f: (layer, 0, 0)),
            pl.BlockSpec((None, d, tf), lambda i, f: (layer, 0, f)),
            pl.BlockSpec((None, d, tf), lambda i, f: (layer, 0, f + nf)),
            pl.BlockSpec((None, tf, d), lambda i, f: (layer, f, 0)),
        ],
        out_specs=pl.BlockSpec((tm, d), lambda i, f: (i, 0)),
        out_shape=jax.ShapeDtypeStruct((m, d), F32),
        scratch_shapes=[pltpu.VMEM((tm, d), BF16)],
        compiler_params=_params("arbitrary", "arbitrary", vmem_limit_bytes=FFN_VMEM_LIMIT_BYTES),
        name="ffn",
    )(x, mod, mod, mod, gain, w_in, w_in, w_out)


def _inproj_kernel(x_ref, sh_ref, sc_ref, gain_ref, w_ref, o_ref, h_ref):
    @pl.when(pl.program_id(1) == 0)
    def _():
        h_ref[...] = _mod_norm(x_ref[...], gain_ref[...], sc_ref[...], sh_ref[...]).astype(BF16)

    o_ref[...] = _dot_nt(h_ref[...], w_ref[...])


def _inproj(x, mod, layer, gain, w_r, tm, tn):
    m, d = x.shape
    nr = w_r.shape[1]
    tiles_per_row = (m // mod.shape[1]) // tm
    return pl.pallas_call(
        _inproj_kernel,
        grid=(m // tm, nr // tn),
        in_specs=[
            pl.BlockSpec((tm, d), lambda i, j: (i, 0)),
            _mod_spec(mod, layer, 3, d, tiles_per_row),
            _mod_spec(mod, layer, 4, d, tiles_per_row),
            pl.BlockSpec((None, 1, d), lambda i, j: (layer, 0, 0)),
            pl.BlockSpec((None, tn, d), lambda i, j: (layer, j, 0)),
        ],
        out_specs=pl.BlockSpec((tm, tn), lambda i, j: (i, j)),
        out_shape=jax.ShapeDtypeStruct((m, nr), F32),
        scratch_shapes=[pltpu.VMEM((tm, d), BF16)],
        compiler_params=_params("arbitrary", "arbitrary"),
        name="inproj",
    )(x, mod, mod, gain, w_r)


def _unit_lower_inverses(lmats, ri, ci):
    c = lmats[0].shape[0]
    n = len(lmats)
    eye = (ri == ci).astype(F32)
    if c == SUBLANES:
        ts = [eye for _ in lmats]
        for j in range(c - 1):
            ts = [t - l[:, j:j + 1] * t[j:j + 1, :] for t, l in zip(ts, lmats)]
        return ts
    pair = (jnp.right_shift(ri, 1) == jnp.right_shift(ci, 1)) & (ri == ci + 1)
    ts = [eye - jnp.where(pair, l, 0.0) for l in lmats]
    shift = 1
    while (1 << shift) < c:
        rb = jnp.right_shift(ri, shift)
        cb = jnp.right_shift(ci, shift)
        off = (jnp.right_shift(rb, 1) == jnp.right_shift(cb, 1)) & (rb == cb + 1)
        t16 = [t.astype(BF16) for t in ts]
        ps = [_dot(t16[h], jnp.where(off, lmats[h], 0.0).astype(BF16)).astype(BF16) for h in range(n)]
        xs = [_dot(ps[h], t16[h]) for h in range(n)]
        ts = [ts[h] - xs[h] for h in range(n)]
        shift += 1
    return ts


def _gdn_kernel(*refs, chunk, chunks_per_step, has_init):
    if has_init:
        (xc_ref, z_ref, bd_ref, cw_ref, alog_ref, dt_ref, on_ref, conv0_ref, s0_ref,
         o_ref, s_ref, conv_ref, xs_ref) = refs
    else:
        (xc_ref, z_ref, bd_ref, cw_ref, alog_ref, dt_ref, on_ref,
         o_ref, s_ref, conv_ref, xs_ref) = refs
    t = pl.program_id(1)
    nt = pl.num_programs(1)
    c = chunk
    tb = c * chunks_per_step
    hist = SUBLANES

    @pl.when(t == 0)
    def _():
        xs_ref[0:hist, :] = jnp.zeros((hist, DN_CONV_CH), F32)
        if has_init:
            xs_ref[hist - (DN_CONV - 1):hist, :] = conv0_ref[...]
            s_ref[...] = s0_ref[...]
        else:
            s_ref[...] = jnp.zeros_like(s_ref)

    xs_ref[hist:hist + tb, :] = xc_ref[...]
    acc = xs_ref[hist:hist + tb, :] * cw_ref[DN_CONV - 1:DN_CONV, :]
    for i in range(DN_CONV - 1):
        off = hist - (DN_CONV - 1) + i
        acc = acc + xs_ref[off:off + tb, :] * cw_ref[i:i + 1, :]
    xc_all = _silu(acc)

    @pl.when(t == nt - 1)
    def _():
        conv_ref[...] = xs_ref[hist + tb - (DN_CONV - 1):hist + tb, :]

    xs_ref[0:hist, :] = xs_ref[tb:tb + hist, :]

    bd = bd_ref[...]
    beta_blk = _sigmoid(bd)
    g_blk = -jnp.exp(alog_ref[...]) * _softplus(bd + dt_ref[...])
    ri = lax.broadcasted_iota(jnp.int32, (c, c), 0)
    ci = lax.broadcasted_iota(jnp.int32, (c, c), 1)
    incl = ri >= ci
    strict = ri > ci
    tri = incl.astype(F32)

    on = on_ref[...]
    heads = range(DN_HEADS)
    pairs = [(j, h) for j in range(chunks_per_step) for h in heads]
    q_all = xc_all[:, 0:DN_QK]
    k_all = xc_all[:, DN_QK:2 * DN_QK]
    if tb >= DN_CHUNK:
        q_scale = _seg_rms_scale(q_all, DN_HEADS, mean=False)
        k_scale = _seg_rms_scale(k_all, DN_HEADS, mean=False)
    else:
        q_scale, k_scale = (jnp.concatenate(
            [jnp.broadcast_to(lax.rsqrt(jnp.sum(jnp.square(a[:, DN_DK * h:DN_DK * (h + 1)]), axis=-1, keepdims=True)
                                        + NORM_EPS), (tb, DN_DK)) for h in heads], axis=-1) for a in (q_all, k_all))
    qn_all = q_all * (q_scale * DN_SCALE)
    kn_all = k_all * k_scale
    qs, ks, vs, betas, gcols, glasts, decays, egcs = [], [], [], [], [], [], [], []
    for j in range(chunks_per_step):
        rows = slice(c * j, c * (j + 1))
        xc = xc_all[rows, :]
        qn = qn_all[rows, :]
        kn = kn_all[rows, :]
        beta_all = beta_blk[rows, :]
        gc_all = _dot(tri, g_blk[rows, :], precision=lax.Precision.HIGHEST)
        gc_pad = gc_all if c == LANES else jnp.concatenate([gc_all, jnp.zeros((LANES - c, LANES), F32)], axis=0)
        gc_t = gc_pad.T
        for h in heads:
            qs.append(qn[:, DN_DK * h:DN_DK * (h + 1)])
            ks.append(kn[:, DN_DK * h:DN_DK * (h + 1)])
            vs.append(xc[:, 2 * DN_QK + DN_DV * h:2 * DN_QK + DN_DV * (h + 1)])
            betas.append(beta_all[:, h:h + 1])
            gcol = gc_all[:, DN_HEADS + h:DN_HEADS + h + 1]
            grow = gc_t[DN_HEADS + h:DN_HEADS + h + 1, 0:c]
            gcols.append(gcol)
            glasts.append(gc_all[c - 1:c, DN_HEADS + h:DN_HEADS + h + 1])
            decays.append(jnp.exp(jnp.where(incl, gcol - grow, NEG_INF)))
            egcs.append(jnp.exp(gcol))
    n = len(pairs)
    kbs = [ks[i] * betas[i] for i in range(n)]
    k16 = [ks[i].astype(BF16) for i in range(n)]
    kk = [_dot_nt(kbs[i].astype(BF16), k16[i]) for i in range(n)]
    qk = [_dot_nt(qs[i].astype(BF16), k16[i]) for i in range(n)]
    lmats = [jnp.where(strict, kk[i] * decays[i], 0.0) for i in range(n)]
    tinv = _unit_lower_inverses(lmats, ri, ci)
    rhs = [jnp.concatenate([vs[i] * betas[i], kbs[i] * egcs[i]], axis=-1).astype(BF16) for i in range(n)]
    sol = [_dot(tinv[i].astype(BF16), rhs[i]) for i in range(n)]
    wq = [jnp.concatenate([sol[i][:, DN_DV:], qs[i] * egcs[i]], axis=0).astype(BF16) for i in range(n)]
    a16 = [(qk[i] * decays[i]).astype(BF16) for i in range(n)]
    kdec = [(ks[i] * jnp.exp(glasts[i] - gcols[i])).astype(BF16) for i in range(n)]
    state = [s_ref[h] for h in heads]
    for j in range(chunks_per_step):
        idx = [j * DN_HEADS + h for h in heads]
        s16 = [s.astype(BF16) for s in state]
        wqs = [_dot(wq[i], s16[h]) for h, i in zip(heads, idx)]
        v16 = [(sol[i][:, :DN_DV] - wqs[h][:c, :]).astype(BF16) for h, i in zip(heads, idx)]
        av = [_dot(a16[i], v16[h]) for h, i in zip(heads, idx)]
        kv = [_dot_tn(kdec[i], v16[h]) for h, i in zip(heads, idx)]
        state = [state[h] * jnp.exp(glasts[i]) + kv[h] for h, i in zip(heads, idx)]
        for h in heads:
            o = wqs[h][c:, :] + av[h]
            o = o * lax.rsqrt(jnp.mean(o * o, axis=-1, keepdims=True) + NORM_EPS) * on
            zh = z_ref[c * j:c * (j + 1), DN_DV * h:DN_DV * (h + 1)]
            o_ref[c * j:c * (j + 1), DN_DV * h:DN_DV * (h + 1)] = (o * _silu(zh)).astype(o_ref.dtype)
    for h in heads:
        s_ref[h] = state[h]


def _gdn(y, batch, layer, conv_w, alog_row, dt_row, out_norm, conv0, s0, chunk, chunks_per_step, out_dtype):
    m = y.shape[0]
    seq = m // batch
    tb = chunk * chunks_per_step
    nt = seq // tb
    has_init = conv0 is not None
    row = lambda b, t: b * nt + t
    in_specs = [
        pl.BlockSpec((tb, DN_CONV_CH), lambda b, t: (row(b, t), COL_CONV // DN_CONV_CH)),
        pl.BlockSpec((tb, DN_V), lambda b, t: (row(b, t), COL_Z // DN_V)),
        pl.BlockSpec((tb, BD_WIDTH), lambda b, t: (row(b, t), y.shape[1] // BD_WIDTH - 1 - _BD_TAIL_BLOCKS)),
        pl.BlockSpec((None, DN_CONV, DN_CONV_CH), lambda b, t: (layer, 0, 0)),
        pl.BlockSpec((None, 1, LANES), lambda b, t: (layer, 0, 0)),
        pl.BlockSpec((None, 1, LANES), lambda b, t: (layer, 0, 0)),
        pl.BlockSpec((None, 1, DN_DV), lambda b, t: (layer, 0, 0)),
    ]
    args = [y, y, y, conv_w, alog_row, dt_row, out_norm]
    if has_init:
        in_specs += [
            pl.BlockSpec((None, None, DN_CONV - 1, DN_CONV_CH), lambda b, t: (layer, b, 0, 0)),
            pl.BlockSpec((None, None, DN_HEADS, DN_DK, DN_DV), lambda b, t: (layer, b, 0, 0, 0)),
        ]
        args += [conv0, s0]
    return pl.pallas_call(
        functools.partial(_gdn_kernel, chunk=chunk, chunks_per_step=chunks_per_step, has_init=has_init),
        grid=(batch, nt),
        in_specs=in_specs,
        out_specs=[
            pl.BlockSpec((tb, DN_V), lambda b, t: (row(b, t), 0)),
            pl.BlockSpec((None, DN_HEADS, DN_DK, DN_DV), lambda b, t: (b, 0, 0, 0)),
            pl.BlockSpec((None, DN_CONV - 1, DN_CONV_CH), lambda b, t: (b, 0, 0)),
        ],
        out_shape=[
            jax.ShapeDtypeStruct((m, DN_V), out_dtype),
            jax.ShapeDtypeStruct((batch, DN_HEADS, DN_DK, DN_DV), F32),
            jax.ShapeDtypeStruct((batch, DN_CONV - 1, DN_CONV_CH), F32),
        ],
        scratch_shapes=[pltpu.VMEM((tb + SUBLANES, DN_CONV_CH), F32)],
        compiler_params=_params("arbitrary", "arbitrary"),
        name="gdn",
    )(*args)


def _seg_rms_scale(x, heads, mean=True):
    w = x.shape[1]
    hd = w // heads
    shift = int(math.log2(hd))
    norm = 1.0 / hd if mean else 1.0
    seg = (jnp.right_shift(lax.broadcasted_iota(jnp.int32, (w, LANES), 0), shift)
           == lax.broadcasted_iota(jnp.int32, (w, LANES), 1)).astype(BF16)
    seg_t = (lax.broadcasted_iota(jnp.int32, (LANES, w), 0)
             == jnp.right_shift(lax.broadcasted_iota(jnp.int32, (LANES, w), 1), shift)).astype(BF16)
    x2 = x * x
    x2_hi = x2.astype(BF16)
    x2_lo = (x2 - x2_hi.astype(F32)).astype(BF16)
    scale = lax.rsqrt((_dot(x2_hi, seg) + _dot(x2_lo, seg)) * norm + NORM_EPS)
    hi = scale.astype(BF16)
    lo = (scale - hi.astype(F32)).astype(BF16)
    return _dot(hi, seg_t) + _dot(lo, seg_t)


def _same_shape_fold(xs, op):
    acc = {}
    for x in xs:
        acc[x.shape] = x if x.shape not in acc else op(acc[x.shape], x)
    return list(acc.values())


def _attn_groups(groups):
    tq = groups[0][0][0].shape[0]
    qg = [jnp.concatenate(g[0], axis=0).astype(BF16) for g in groups]
    raw = [[_dot_nt(q, k) for k in g[1]] for q, g in zip(qg, groups)]
    probs, sink_terms = [], []
    for r, g in zip(raw, groups):
        sink = g[4]
        ss = [s + b for s, b in zip(r, g[3])]
        m = sink
        for s in _same_shape_fold(ss, jnp.maximum):
            m = jnp.maximum(m, jnp.max(s, axis=-1, keepdims=True))
        probs.append([jnp.exp(s - m).astype(BF16) for s in ss])
        sink_terms.append(jnp.exp(sink - m))
    pv = [[_dot(p, v) for p, v in zip(ps, g[2])] for ps, g in zip(probs, groups)]
    rowsum = [[_dot(p, jnp.ones((p.shape[1], LANES), BF16)) for p in ps] for ps in probs]
    outs = []
    for parts, sums, es in zip(pv, rowsum, sink_terms):
        o = parts[0]
        for part in parts[1:]:
            o = o + part
        den = sums[0]
        for part in sums[1:]:
            den = den + part
        o = o * (1.0 / (den + es))[:, :o.shape[1]]
        outs += [o[tq * j:tq * (j + 1), :] for j in range(SW_GROUP)]
    return outs


def _head_cols(x, heads):
    return [x[:, SW_HD * h:SW_HD * (h + 1)] for h in heads]


def _swa_prompt_kernel(q_ref, kp_ref, kc_ref, vp_ref, vc_ref, kq_ref, kn_ref, sink_ref, bias_ref,
                       o_ref, kout_ref, vout_ref, *, qb):
    n = pl.program_id(1)
    nb = pl.num_programs(1)
    kv_heads = range(SW_KV_HEADS)
    q = q_ref[...]
    qn = q * _seg_rms_scale(q, SW_HEADS)
    kc = kc_ref[...]
    kc = kc * _seg_rms_scale(kc, SW_KV_HEADS) * kn_ref[...]
    kp = kp_ref[...]
    kp = kp * _seg_rms_scale(kp, SW_KV_HEADS) * kn_ref[...]
    k_all = jnp.concatenate([kp, kc], axis=0) * kq_ref[...]
    vc = vc_ref[...]
    v_all = jnp.concatenate([vp_ref[...], vc], axis=0)
    first = n == 0
    groups = []
    for s in range(qb):
        blk = lambda a, i: a[SW_BLOCK * i:SW_BLOCK * (i + 1), :]
        qs = blk(qn, s)
        ks = [_head_cols(blk(k_all, s + i), kv_heads) for i in range(2)]
        vs = [_head_cols(blk(v_all, s + i), kv_heads) for i in range(2)]
        for g in kv_heads:
            bias = bias_ref[g]
            bias_prev = bias[:, :SW_BLOCK]
            if s == 0:
                bias_prev = jnp.where(first, NEG_INF, bias_prev)
            groups.append((
                _head_cols(qs, range(SW_GROUP * g, SW_GROUP * (g + 1))),
                [ks[0][g].astype(BF16), ks[1][g].astype(BF16)],
                [vs[0][g].astype(BF16), vs[1][g].astype(BF16)],
                [bias_prev, bias[:, SW_BLOCK:]],
                sink_ref[g]))
    outs = _attn_groups(groups)
    for s in range(qb):
        o_ref[SW_BLOCK * s:SW_BLOCK * (s + 1), :] = jnp.concatenate(
            outs[SW_HEADS * s:SW_HEADS * (s + 1)], axis=-1).astype(o_ref.dtype)

    @pl.when(n == nb - 1)
    def _():
        kout_ref[...] = kc[SW_BLOCK * (qb - 1):, :]
        vout_ref[...] = vc[SW_BLOCK * (qb - 1):, :]


def _swa_prompt(y, batch, layer, q_norm, k_norm, sink_col, bias, col_q, out_dtype, qb):
    m = y.shape[0]
    seq = m // batch
    tq = SW_BLOCK * qb
    nb = seq // tq
    cq = col_q // SW_Q
    ck = (col_q + SW_Q) // SW_KV
    cv = ck + 1
    cur = lambda b, n: b * nb + n
    prev = lambda b, n: (b * nb + n) * qb - jnp.minimum(n, 1)
    return pl.pallas_call(
        functools.partial(_swa_prompt_kernel, qb=qb),
        grid=(batch, nb),
        in_specs=[
            pl.BlockSpec((tq, SW_Q), lambda b, n: (cur(b, n), cq)),
            pl.BlockSpec((SW_BLOCK, SW_KV), lambda b, n: (prev(b, n), ck)),
            pl.BlockSpec((tq, SW_KV), lambda b, n: (cur(b, n), ck)),
            pl.BlockSpec((SW_BLOCK, SW_KV), lambda b, n: (prev(b, n), cv)),
            pl.BlockSpec((tq, SW_KV), lambda b, n: (cur(b, n), cv)),
            pl.BlockSpec((None, 1, SW_KV), lambda b, n: (layer, 0, 0)),
            pl.BlockSpec((None, 1, SW_KV), lambda b, n: (layer, 0, 0)),
            pl.BlockSpec((None, SW_KV_HEADS, SW_GROUP * SW_BLOCK, 1), lambda b, n: (layer, 0, 0, 0)),
            pl.BlockSpec((SW_KV_HEADS, SW_GROUP * SW_BLOCK, 2 * SW_BLOCK), lambda b, n: (0, 0, 0)),
        ],
        out_specs=[
            pl.BlockSpec((tq, SW_Q), lambda b, n: (cur(b, n), 0)),
            pl.BlockSpec((None, WINDOW, SW_KV), lambda b, n: (b, 0, 0)),
            pl.BlockSpec((None, WINDOW, SW_KV), lambda b, n: (b, 0, 0)),
        ],
        out_shape=[
            jax.ShapeDtypeStruct((m, SW_Q), out_dtype),
            jax.ShapeDtypeStruct((batch, WINDOW, SW_KV), F32),
            jax.ShapeDtypeStruct((batch, WINDOW, SW_KV), F32),
        ],
        compiler_params=_params("arbitrary", "arbitrary"),
        name="swa_prompt",
    )(y, y, y, y, y, q_norm, k_norm, sink_col, bias)


def _swa_sample_kernel(q_ref, k_ref, v_ref, kbuf_ref, vbuf_ref, kq_ref, kn_ref, sink_ref, bb_ref, bn_ref,
                       o_ref, kout_ref, vout_ref, *, bs):
    t = q_ref.shape[1]
    wc = kbuf_ref.shape[1]
    kv_heads = range(SW_KV_HEADS)
    groups = []
    for i in range(bs):
        q = q_ref[i]
        qn = q * _seg_rms_scale(q, SW_HEADS)
        kn = k_ref[i]
        kn = kn * _seg_rms_scale(kn, SW_KV_HEADS) * kn_ref[...]
        v = v_ref[i]
        kbuf = kbuf_ref[i]
        vbuf = vbuf_ref[i]
        kbs = _head_cols(kbuf * kq_ref[...], kv_heads)
        kns = _head_cols(kn * kq_ref[...], kv_heads)
        vbs = _head_cols(vbuf, kv_heads)
        vns = _head_cols(v, kv_heads)
        for g in kv_heads:
            groups.append((
                _head_cols(qn, range(SW_GROUP * g, SW_GROUP * (g + 1))),
                [kbs[g].astype(BF16), kns[g].astype(BF16)],
                [vbs[g].astype(BF16), vns[g].astype(BF16)],
                [bb_ref[g], bn_ref[g]],
                sink_ref[g]))
        kout_ref[i, 0:wc - t, :] = kbuf[t:, :]
        kout_ref[i, wc - t:wc, :] = kn
        vout_ref[i, 0:wc - t, :] = vbuf[t:, :]
        vout_ref[i, wc - t:wc, :] = v
    outs = _attn_groups(groups)
    for i in range(bs):
        o_ref[i] = jnp.concatenate(outs[SW_HEADS * i:SW_HEADS * (i + 1)], axis=-1).astype(o_ref.dtype)


def _swa_sample(y3, layer, k_buf, v_buf, q_norm, k_norm, sink_col, bias_buf, bias_new, col_q, bs):
    batch, t, _ = y3.shape
    wc = k_buf.shape[2]
    cq = col_q // SW_Q
    ck = (col_q + SW_Q) // SW_KV
    cv = ck + 1
    return pl.pallas_call(
        functools.partial(_swa_sample_kernel, bs=bs),
        grid=(batch // bs,),
        in_specs=[
            pl.BlockSpec((bs, t, SW_Q), lambda b: (b, 0, cq)),
            pl.BlockSpec((bs, t, SW_KV), lambda b: (b, 0, ck)),
            pl.BlockSpec((bs, t, SW_KV), lambda b: (b, 0, cv)),
            pl.BlockSpec((None, bs, wc, SW_KV), lambda b: (layer, b, 0, 0)),
            pl.BlockSpec((None, bs, wc, SW_KV), lambda b: (layer, b, 0, 0)),
            pl.BlockSpec((None, 1, SW_KV), lambda b: (layer, 0, 0)),
            pl.BlockSpec((None, 1, SW_KV), lambda b: (layer, 0, 0)),
            pl.BlockSpec((None, SW_KV_HEADS, SW_GROUP * t, 1), lambda b: (layer, 0, 0, 0)),
            pl.BlockSpec((SW_KV_HEADS, SW_GROUP * t, wc), lambda b: (0, 0, 0)),
            pl.BlockSpec((SW_KV_HEADS, SW_GROUP * t, t), lambda b: (0, 0, 0)),
        ],
        out_specs=[
            pl.BlockSpec((bs, t, SW_Q), lambda b: (b, 0, 0)),
            pl.BlockSpec((bs, wc, SW_KV), lambda b: (b, 0, 0)),
            pl.BlockSpec((bs, wc, SW_KV), lambda b: (b, 0, 0)),
        ],
        out_shape=[
            jax.ShapeDtypeStruct((batch, t, SW_Q), F32),
            jax.ShapeDtypeStruct((batch, wc, SW_KV), F32),
            jax.ShapeDtypeStruct((batch, wc, SW_KV), F32),
        ],
        compiler_params=_params("arbitrary"),
        name="swa_sample",
    )(y3, y3, y3, k_buf, v_buf, q_norm, k_norm, sink_col, bias_buf, bias_new)


def _mixout_kernel(x_ref, oa_ref, ob_ref, ga_ref, gb_ref, g_ref, wdn_ref, wsw_ref, wo_ref, o_ref):
    ya = _dot(oa_ref[...].astype(BF16), wdn_ref[...])
    yb = _dot(ob_ref[...].astype(BF16), wsw_ref[...])
    merged = _sigmoid(ga_ref[...]) * ya + _sigmoid(gb_ref[...]) * yb
    o_ref[...] = x_ref[...] + g_ref[...] * _dot(merged.astype(BF16), wo_ref[...])


def _mixout(x, o_a, o_b, y, mod, layer, w_dn, w_sw, w_o, tm):
    m, d = x.shape
    tiles_per_row = (m // mod.shape[1]) // tm
    cga = COL_GA // d
    const = dict(pipeline_mode=pl.Buffered(1))
    return pl.pallas_call(
        _mixout_kernel,
        grid=(m // tm,),
        in_specs=[
            pl.BlockSpec((tm, d), lambda i: (i, 0)),
            pl.BlockSpec((tm, DN_V), lambda i: (i, 0)),
            pl.BlockSpec((tm, SW_Q), lambda i: (i, 0)),
            pl.BlockSpec((tm, d), lambda i: (i, cga)),
            pl.BlockSpec((tm, d), lambda i: (i, cga + 1)),
            pl.BlockSpec((None, None, mod.shape[2], d), lambda i: (layer, i // tiles_per_row, 0, 5)),
            pl.BlockSpec((None, DN_V, d), lambda i: (layer, 0, 0), **const),
            pl.BlockSpec((None, SW_Q, d), lambda i: (layer, 0, 0), **const),
            pl.BlockSpec((None, d, d), lambda i: (layer, 0, 0), **const),
        ],
        out_specs=pl.BlockSpec((tm, d), lambda i: (i, 0)),
        out_shape=jax.ShapeDtypeStruct((m, d), F32),
        compiler_params=_params("arbitrary"),
        name="mixout",
    )(x, o_a, o_b, y, y, mod, w_dn, w_sw, w_o)


_BD_TAIL_BLOCKS = 3
TM_PROMPT = 1024
TF = 256
TF_SAMPLE = 512
TN = 1024
TM_MIX = 256
GDN_CHUNKS_PER_STEP = 2
SWA_Q_BLOCKS_PER_STEP = 2
SAMPLE_BATCH_PER_STEP = 4


def _reorder_kernel(w_ref, o_ref, *, segments):
    row = 0
    for start, height in segments:
        o_ref[row:row + height, :] = w_ref[start:start + height, :].astype(o_ref.dtype)
        row += height
    o_ref[row:, :] = jnp.zeros((o_ref.shape[0] - row, o_ref.shape[1]), o_ref.dtype)


def _reorder_w_in(w_in, d, tk=256):
    sizes = (DN_CONV_CH, DN_V, DN_HEADS, DN_HEADS, SW_Q, SW_KV, SW_KV, d, d)
    o = [int(v) for v in np.concatenate([[0], np.cumsum(sizes)])]
    segments = ((0, o[2]), (o[7], 2 * d), (o[4], o[7] - o[4]), (o[2], o[4] - o[2]))
    w_t = jnp.swapaxes(w_in, 1, 2)
    depth, n_in, rows = w_t.shape
    n_out = sum(h for _, h in segments) + BD_WIDTH - 2 * DN_HEADS + _BD_TAIL_BLOCKS * LANES
    return pl.pallas_call(
        functools.partial(_reorder_kernel, segments=segments),
        grid=(depth, rows // tk),
        in_specs=[pl.BlockSpec((None, n_in, tk), lambda l, i: (l, 0, i))],
        out_specs=pl.BlockSpec((None, n_out, tk), lambda l, i: (l, 0, i)),
        out_shape=jax.ShapeDtypeStruct((depth, n_out, rows), BF16),
        compiler_params=_params("arbitrary", "arbitrary"),
        name="reorder_w_in",
    )(w_t)


def kernel(x_prompt, x_sample, c_prompt, c_sample, state_delta, state_conv, cache_swa_k, cache_swa_v, rel_bias, w_ada, b_ada, norm_ffn1, w_ffn1_in, w_ffn1_out, norm_mix, w_in, dn_conv_w, dn_a_log, dn_dt_bias, dn_out_norm, w_dn_out, sw_q_norm, sw_k_norm, sw_sinks, w_sw_out, w_o, norm_ffn2, w_ffn2_in, w_ffn2_out):
    bp, seq, d = x_prompt.shape
    bd, dseq, _ = x_sample.shape
    depth = w_ada.shape[0]
    wc = cache_swa_k.shape[2]
    col_q = COL_GA + 2 * d

    n_rows = bp + bd
    pad_rows = -n_rows % SUBLANES
    c_all = jnp.concatenate([c_prompt, c_sample, jnp.zeros((pad_rows, d), F32)], axis=0)
    mod_all = _ada(c_all, w_ada, b_ada)
    mod_p = mod_all[:, :bp].reshape(depth, bp, 1, N_MOD * d)
    mod_s = jnp.repeat(mod_all[:, bp:n_rows], dseq, axis=1).reshape(depth, 1, bd * dseq, N_MOD * d)

    qi = np.arange(SW_BLOCK)[:, None]
    kj = np.arange(2 * SW_BLOCK)[None, :]
    bias_p = _bias_grid(rel_bias, _bucket_map(qi + SW_BLOCK - kj))
    ti = np.arange(dseq)[:, None]
    sj = np.arange(wc + dseq)[None, :]
    dist_s = ti + wc - sj
    bias_sb = _bias_grid(rel_bias, _bucket_map(dist_s[:, :wc]))
    bias_sn = _bias_grid(rel_bias, _bucket_map(dist_s[:, wc:]))

    w_in_r = _reorder_w_in(w_in, d)
    w_dn16 = w_dn_out.astype(BF16)
    w_sw16 = w_sw_out.astype(BF16)
    w_o16 = w_o.astype(BF16)
    r3 = lambda a: a.reshape(depth, 1, a.shape[-1])
    lane_row = lambda a: jnp.pad(a, ((0, 0), (DN_HEADS, LANES - 2 * DN_HEADS))).reshape(depth, 1, LANES)
    alog_row = lane_row(dn_a_log)
    dt_row = lane_row(dn_dt_bias)
    sinks = sw_sinks.reshape(depth, SW_KV_HEADS, SW_GROUP, 1)
    sink_p = jnp.repeat(sinks, SW_BLOCK, axis=2)
    sink_s = jnp.repeat(sinks, dseq, axis=2)
    kbuf = cache_swa_k.reshape(depth, bd, wc, SW_KV)
    vbuf = cache_swa_v.reshape(depth, bd, wc, SW_KV)
    g_ffn1, g_mix, g_ffn2, g_on = r3(norm_ffn1), r3(norm_mix), r3(norm_ffn2), r3(dn_out_norm)
    g_qn = r3(jnp.tile(sw_q_norm * SW_SCALE, (1, SW_KV_HEADS)))
    g_kn = r3(jnp.tile(sw_k_norm, (1, SW_KV_HEADS)))

    xp = x_prompt.reshape(bp * seq, d)
    xs = x_sample.reshape(bd * dseq, d)
    ms = bd * dseq
    outs = [[] for _ in range(8)]
    for l in range(depth):
        xp = _ffn(xp, mod_p, 0, l, g_ffn1, w_ffn1_in, w_ffn1_out, TM_PROMPT, TF)
        yp = _inproj(xp, mod_p, l, g_mix, w_in_r, TM_PROMPT, TN)
        oa, s_p, conv_p = _gdn(yp, bp, l, dn_conv_w, alog_row, dt_row, g_on, None, None, DN_CHUNK,
                               GDN_CHUNKS_PER_STEP, BF16)
        ob, k_p, v_p = _swa_prompt(yp, bp, l, g_qn, g_kn, sink_p, bias_p, col_q, BF16, SWA_Q_BLOCKS_PER_STEP)
        xp = _mixout(xp, oa, ob, yp, mod_p, l, w_dn16, w_sw16, w_o16, TM_MIX)
        xp = _ffn(xp, mod_p, 6, l, g_ffn2, w_ffn2_in, w_ffn2_out, TM_PROMPT, TF)
        xs = _ffn(xs, mod_s, 0, l, g_ffn1, w_ffn1_in, w_ffn1_out, ms, TF_SAMPLE)
        ys = _inproj(xs, mod_s, l, g_mix, w_in_r, ms, TN)
        oa, s_s, conv_s = _gdn(ys, bd, l, dn_conv_w, alog_row, dt_row, g_on, state_conv, state_delta, dseq, 1, F32)
        ob, k_s, v_s = _swa_sample(ys.reshape(bd, dseq, -1), l, kbuf, vbuf, g_qn, g_kn, sink_s, bias_sb, bias_sn,
                                   col_q, SAMPLE_BATCH_PER_STEP)
        xs = _mixout(xs, oa, ob.reshape(ms, SW_Q), ys, mod_s, l, w_dn16, w_sw16, w_o16, ms)
        xs = _ffn(xs, mod_s, 6, l, g_ffn2, w_ffn2_in, w_ffn2_out, ms, TF_SAMPLE)
        for lst, val in zip(outs, (s_p, conv_p, k_p, v_p, s_s, conv_s, k_s, v_s)):
            lst.append(val)
    s_p, conv_p, k_p, v_p, s_s, conv_s, k_s, v_s = (jnp.stack(o) for o in outs)
    kv = lambda a: a.reshape(a.shape[:3] + (SW_KV_HEADS, SW_HD))
    return (xp.reshape(bp, seq, d), xs.reshape(bd, dseq, d), s_p, conv_p, kv(k_p), kv(v_p),
            s_s, conv_s, kv(k_s), kv(v_s))
```

```python
import functools
import math

import numpy as np
import jax
import jax.numpy as jnp
from jax import lax
from jax.experimental import pallas as pl
from jax.experimental.pallas import tpu as pltpu

F32 = jnp.float32
BF16 = jnp.bfloat16

NORM_EPS = 1e-6
NEG_INF = -1e30
N_MOD = 9

DN_HEADS = 8
DN_DK = 128
DN_DV = 128
DN_CONV = 4
DN_CHUNK = 64
DN_QK = DN_HEADS * DN_DK
DN_V = DN_HEADS * DN_DV
DN_CONV_CH = 2 * DN_QK + DN_V
DN_SCALE = DN_DK ** -0.5

SW_HEADS = 16
SW_KV_HEADS = 4
SW_HD = 64
SW_GROUP = SW_HEADS // SW_KV_HEADS
SW_SCALE = SW_HD ** -0.5
WINDOW = 128
SW_BLOCK = 128
SW_Q = SW_HEADS * SW_HD
SW_KV = SW_KV_HEADS * SW_HD

N_BUCKETS = 32
MAX_DISTANCE = 128

LANES = 128
SUBLANES = 8
VMEM_LIMIT_BYTES = 56 * 1024 * 1024
FFN_VMEM_LIMIT_BYTES = 60 * 1024 * 1024

COL_CONV = 0
COL_Z = COL_CONV + DN_CONV_CH
COL_GA = COL_Z + DN_V
BD_WIDTH = LANES


def _sigmoid(x):
    return 1.0 / (1.0 + jnp.exp(-x))


def _silu(x):
    return x * _sigmoid(x)


def _softplus(x):
    return jnp.maximum(x, 0.0) + jnp.log1p(jnp.exp(-jnp.abs(x)))


def _dot(a, b, **kw):
    return jnp.dot(a, b, preferred_element_type=F32, **kw)


def _dot_nt(a, b):
    return lax.dot_general(a, b, (((1,), (1,)), ((), ())), preferred_element_type=F32)


def _dot_tn(a, b):
    return lax.dot_general(a, b, (((0,), (0,)), ((), ())), preferred_element_type=F32)


def _params(*semantics, vmem_limit_bytes=VMEM_LIMIT_BYTES):
    return pltpu.CompilerParams(dimension_semantics=semantics, vmem_limit_bytes=vmem_limit_bytes)


def _mod_norm(x, gain, sc, sh):
    r = lax.rsqrt(jnp.mean(x * x, axis=-1, keepdims=True) + NORM_EPS)
    if sc.shape[0] == 1:
        return (x * r) * (gain * (1.0 + sc)) + sh
    return x * r * gain * (1.0 + sc) + sh


def _ada_kernel(c_ref, w_ref, b_ref, o_ref):
    a = _silu(c_ref[...]).astype(BF16)
    o_ref[...] = _dot(a, w_ref[...].astype(BF16)) + b_ref[...]


def _ada(c_all, w_ada, b_ada, tn=1024):
    depth, d, n = w_ada.shape
    rows = c_all.shape[0]
    return pl.pallas_call(
        _ada_kernel,
        grid=(depth, n // tn),
        in_specs=[
            pl.BlockSpec((rows, d), lambda l, j: (0, 0)),
            pl.BlockSpec((None, d, tn), lambda l, j: (l, 0, j)),
            pl.BlockSpec((None, 1, tn), lambda l, j: (l, 0, j)),
        ],
        out_specs=pl.BlockSpec((None, rows, tn), lambda l, j: (l, 0, j)),
        out_shape=jax.ShapeDtypeStruct((depth, rows, n), F32),
        compiler_params=_params("arbitrary", "arbitrary"),
        name="ada",
    )(c_all, w_ada, b_ada.reshape(depth, 1, n))


def _bucket_table():
    max_exact = N_BUCKETS // 2
    d = np.arange(WINDOW + 1)
    ratio = np.log(np.maximum(d, 1).astype(np.float32) / np.float32(max_exact)) / np.float32(math.log(MAX_DISTANCE / max_exact))
    large = max_exact + (ratio.astype(np.float32) * np.float32(N_BUCKETS - max_exact)).astype(np.int32)
    return np.where(d < max_exact, d, np.minimum(large, N_BUCKETS - 1)).astype(np.int32)


def _bucket_map(dist):
    table = _bucket_table()
    valid = (dist >= 0) & (dist <= WINDOW)
    return np.where(valid, table[np.clip(dist, 0, WINDOW)], -1).astype(np.int32)


def _bias_kernel(rb_ref, bm_ref, o_ref):
    bm = bm_ref[...]
    tq = bm.shape[0]
    for g in range(SW_KV_HEADS):
        for j in range(SW_GROUP):
            acc = jnp.full(bm.shape, NEG_INF, F32)
            for b in range(N_BUCKETS):
                acc = jnp.where(bm == b, rb_ref[b, g * SW_GROUP + j], acc)
            o_ref[g, j * tq:(j + 1) * tq, :] = acc


def _bias_grid(rel_bias, bmap):
    tq, tk = bmap.shape
    return pl.pallas_call(
        _bias_kernel,
        in_specs=[
            pl.BlockSpec(memory_space=pltpu.SMEM),
            pl.BlockSpec((tq, tk), lambda: (0, 0)),
        ],
        out_specs=pl.BlockSpec((SW_KV_HEADS, SW_GROUP * tq, tk), lambda: (0, 0, 0)),
        out_shape=jax.ShapeDtypeStruct((SW_KV_HEADS, SW_GROUP * tq, tk), F32),
        name="relbias",
    )(rel_bias, jnp.asarray(bmap))


def _ffn_kernel(x_ref, sh_ref, sc_ref, g_ref, gain_ref, wg_ref, wu_ref, wo_ref, o_ref, h_ref, *, nf):
    f = pl.program_id(1)

    @pl.when(f == 0)
    def _():
        h_ref[...] = _mod_norm(x_ref[...], gain_ref[...], sc_ref[...], sh_ref[...]).astype(BF16)
        o_ref[...] = jnp.zeros_like(o_ref)

    h = h_ref[...]
    gate = _dot(h, wg_ref[...].astype(BF16))
    up = _dot(h, wu_ref[...].astype(BF16))
    a = (_silu(gate) * up).astype(BF16)
    o_ref[...] += _dot(a, wo_ref[...].astype(BF16))

    @pl.when(f == nf - 1)
    def _():
        o_ref[...] = x_ref[...] + (0.5 * g_ref[...]) * o_ref[...]


def _mod_spec(mod, layer, seg, d, tiles_per_row):
    r = mod.shape[2]
    return pl.BlockSpec((None, None, r, d), lambda i, j: (layer, i // tiles_per_row, 0, seg))


def _ffn(x, mod, seg0, layer, gain, w_in, w_out, tm, tf):
    m, d = x.shape
    ff = w_out.shape[1]
    nf = ff // tf
    tiles_per_row = (m // mod.shape[1]) // tm
    return pl.pallas_call(
        functools.partial(_ffn_kernel, nf=nf),
        grid=(m // tm, nf),
        in_specs=[
            pl.BlockSpec((tm, d), lambda i, f: (i, 0)),
            _mod_spec(mod, layer, seg0, d, tiles_per_row),
            _mod_spec(mod, layer, seg0 + 1, d, tiles_per_row),
            _mod_spec(mod, layer, seg0 + 2, d, tiles_per_row),
            pl.BlockSpec((None, 1, d), lambda i, f: (layer, 0, 0)),
            pl.BlockSpec((None, d, tf), lambda i, f: (layer, 0, f)),
            pl.BlockSpec((None, d, tf), lambda i, f: (layer, 0, f + nf)),
            pl.BlockSpec((None, tf, d), lambda i, f: (layer, f, 0)),
        ],
        out_specs=pl.BlockSpec((tm, d), lambda i, f: (i, 0)),
        out_shape=jax.ShapeDtypeStruct((m, d), F32),
        scratch_shapes=[pltpu.VMEM((tm, d), BF16)],
        compiler_params=_params("arbitrary", "arbitrary", vmem_limit_bytes=FFN_VMEM_LIMIT_BYTES),
        name="ffn",
    )(x, mod, mod, mod, gain, w_in, w_in, w_out)


def _inproj_kernel(x_ref, sh_ref, sc_ref, gain_ref, w_ref, o_ref, h_ref):
    @pl.when(pl.program_id(1) == 0)
    def _():
        h_ref[...] = _mod_norm(x_ref[...], gain_ref[...], sc_ref[...], sh_ref[...]).astype(BF16)

    o_ref[...] = _dot_nt(h_ref[...], w_ref[...])


def _inproj(x, mod, layer, gain, w_r, tm, tn):
    m, d = x.shape
    nr = w_r.shape[1]
    tiles_per_row = (m // mod.shape[1]) // tm
    return pl.pallas_call(
        _inproj_kernel,
        grid=(m // tm, nr // tn),
        in_specs=[
            pl.BlockSpec((tm, d), lambda i, j: (i, 0)),
            _mod_spec(mod, layer, 3, d, tiles_per_row),
            _mod_spec(mod, layer, 4, d, tiles_per_row),
            pl.BlockSpec((None, 1, d), lambda i, j: (layer, 0, 0)),
            pl.BlockSpec((None, tn, d), lambda i, j: (layer, j, 0)),
        ],
        out_specs=pl.BlockSpec((tm, tn), lambda i, j: (i, j)),
        out_shape=jax.ShapeDtypeStruct((m, nr), F32),
        scratch_shapes=[pltpu.VMEM((tm, d), BF16)],
        compiler_params=_params("arbitrary", "arbitrary"),
        name="inproj",
    )(x, mod, mod, gain, w_r)


def _unit_lower_inverses(lmats, ri, ci):
    c = lmats[0].shape[0]
    n = len(lmats)
    eye = (ri == ci).astype(F32)
    if c == SUBLANES:
        ts = [eye for _ in lmats]
        for j in range(c - 1):
            ts = [t - l[:, j:j + 1] * t[j:j + 1, :] for t, l in zip(ts, lmats)]
        return ts
    pair = (jnp.right_shift(ri, 1) == jnp.right_shift(ci, 1)) & (ri == ci + 1)
    ts = [eye - jnp.where(pair, l, 0.0) for l in lmats]
    shift = 1
    while (1 << shift) < c:
        rb = jnp.right_shift(ri, shift)
        cb = jnp.right_shift(ci, shift)
        off = (jnp.right_shift(rb, 1) == jnp.right_shift(cb, 1)) & (rb == cb + 1)
        t16 = [t.astype(BF16) for t in ts]
        ps = [_dot(t16[h], jnp.where(off, lmats[h], 0.0).astype(BF16)).astype(BF16) for h in range(n)]
        xs = [_dot(ps[h], t16[h]) for h in range(n)]
        ts = [ts[h] - xs[h] for h in range(n)]
        shift += 1
    return ts


def _gdn_kernel(*refs, chunk, chunks_per_step, has_init):
    if has_init:
        (xc_ref, z_ref, bd_ref, cw_ref, alog_ref, dt_ref, on_ref, conv0_ref, s0_ref,
         o_ref, s_ref, conv_ref, xs_ref) = refs
    else:
        (xc_ref, z_ref, bd_ref, cw_ref, alog_ref, dt_ref, on_ref,
         o_ref, s_ref, conv_ref, xs_ref) = refs
    t = pl.program_id(1)
    nt = pl.num_programs(1)
    c = chunk
    tb = c * chunks_per_step
    hist = SUBLANES

    @pl.when(t == 0)
    def _():
        xs_ref[0:hist, :] = jnp.zeros((hist, DN_CONV_CH), F32)
        if has_init:
            xs_ref[hist - (DN_CONV - 1):hist, :] = conv0_ref[...]
            s_ref[...] = s0_ref[...]
        else:
            s_ref[...] = jnp.zeros_like(s_ref)

    xs_ref[hist:hist + tb, :] = xc_ref[...]
    acc = xs_ref[hist:hist + tb, :] * cw_ref[DN_CONV - 1:DN_CONV, :]
    for i in range(DN_CONV - 1):
        off = hist - (DN_CONV - 1) + i
        acc = acc + xs_ref[off:off + tb, :] * cw_ref[i:i + 1, :]
    xc_all = _silu(acc)

    @pl.when(t == nt - 1)
    def _():
        conv_ref[...] = xs_ref[hist + tb - (DN_CONV - 1):hist + tb, :]

    xs_ref[0:hist, :] = xs_ref[tb:tb + hist, :]

    bd = bd_ref[...]
    beta_blk = _sigmoid(bd)
    g_blk = -jnp.exp(alog_ref[...]) * _softplus(bd + dt_ref[...])
    ri = lax.broadcasted_iota(jnp.int32, (c, c), 0)
    ci = lax.broadcasted_iota(jnp.int32, (c, c), 1)
    incl = ri >= ci
    strict = ri > ci
    tri = incl.astype(F32)

    on = on_ref[...]
    heads = range(DN_HEADS)
    pairs = [(j, h) for j in range(chunks_per_step) for h in heads]
    q_all = xc_all[:, 0:DN_QK]
    k_all = xc_all[:, DN_QK:2 * DN_QK]
    if tb >= DN_CHUNK:
        q_scale = _seg_rms_scale(q_all, DN_HEADS, mean=False)
        k_scale = _seg_rms_scale(k_all, DN_HEADS, mean=False)
    else:
        q_scale, k_scale = (jnp.concatenate(
            [jnp.broadcast_to(lax.rsqrt(jnp.sum(jnp.square(a[:, DN_DK * h:DN_DK * (h + 1)]), axis=-1, keepdims=True)
                                        + NORM_EPS), (tb, DN_DK)) for h in heads], axis=-1) for a in (q_all, k_all))
    qn_all = q_all * (q_scale * DN_SCALE)
    kn_all = k_all * k_scale
    qs, ks, vs, betas, gcols, glasts, decays, egcs = [], [], [], [], [], [], [], []
    for j in range(chunks_per_step):
        rows = slice(c * j, c * (j + 1))
        xc = xc_all[rows, :]
        qn = qn_all[rows, :]
        kn = kn_all[rows, :]
        beta_all = beta_blk[rows, :]
        gc_all = _dot(tri, g_blk[rows, :], precision=lax.Precision.HIGHEST)
        gc_pad = gc_all if c == LANES else jnp.concatenate([gc_all, jnp.zeros((LANES - c, LANES), F32)], axis=0)
        gc_t = gc_pad.T
        for h in heads:
            qs.append(qn[:, DN_DK * h:DN_DK * (h + 1)])
            ks.append(kn[:, DN_DK * h:DN_DK * (h + 1)])
            vs.append(xc[:, 2 * DN_QK + DN_DV * h:2 * DN_QK + DN_DV * (h + 1)])
            betas.append(beta_all[:, h:h + 1])
            gcol = gc_all[:, DN_HEADS + h:DN_HEADS + h + 1]
            grow = gc_t[DN_HEADS + h:DN_HEADS + h + 1, 0:c]
            gcols.append(gcol)
            glasts.append(gc_all[c - 1:c, DN_HEADS + h:DN_HEADS + h + 1])
            decays.append(jnp.exp(jnp.where(incl, gcol - grow, NEG_INF)))
            egcs.append(jnp.exp(gcol))
    n = len(pairs)
    kbs = [ks[i] * betas[i] for i in range(n)]
    k16 = [ks[i].astype(BF16) for i in range(n)]
    kk = [_dot_nt(kbs[i].astype(BF16), k16[i]) for i in range(n)]
    qk = [_dot_nt(qs[i].astype(BF16), k16[i]) for i in range(n)]
    lmats = [jnp.where(strict, kk[i] * decays[i], 0.0) for i in range(n)]
    tinv = _unit_lower_inverses(lmats, ri, ci)
    rhs = [jnp.concatenate([vs[i] * betas[i], kbs[i] * egcs[i]], axis=-1).astype(BF16) for i in range(n)]
    sol = [_dot(tinv[i].astype(BF16), rhs[i]) for i in range(n)]
    wq = [jnp.concatenate([sol[i][:, DN_DV:], qs[i] * egcs[i]], axis=0).astype(BF16) for i in range(n)]
    a16 = [(qk[i] * decays[i]).astype(BF16) for i in range(n)]
    kdec = [(ks[i] * jnp.exp(glasts[i] - gcols[i])).astype(BF16) for i in range(n)]
    state = [s_ref[h] for h in heads]
    for j in range(chunks_per_step):
        idx = [j * DN_HEADS + h for h in heads]
        s16 = [s.astype(BF16) for s in state]
        wqs = [_dot(wq[i], s16[h]) for h, i in zip(heads, idx)]
        v16 = [(sol[i][:, :DN_DV] - wqs[h][:c, :]).astype(BF16) for h, i in zip(heads, idx)]
        av = [_dot(a16[i], v16[h]) for h, i in zip(heads, idx)]
        kv = [_dot_tn(kdec[i], v16[h]) for h, i in zip(heads, idx)]
        state = [state[h] * jnp.exp(glasts[i]) + kv[h] for h, i in zip(heads, idx)]
        for h in heads:
            o = wqs[h][c:, :] + av[h]
            o = o * lax.rsqrt(jnp.mean(o * o, axis=-1, keepdims=True) + NORM_EPS) * on
            zh = z_ref[c * j:c * (j + 1), DN_DV * h:DN_DV * (h + 1)]
            o_ref[c * j:c * (j + 1), DN_DV * h:DN_DV * (h + 1)] = (o * _silu(zh)).astype(o_ref.dtype)
    for h in heads:
        s_ref[h] = state[h]


def _gdn(y, batch, layer, conv_w, alog_row, dt_row, out_norm, conv0, s0, chunk, chunks_per_step, out_dtype):
    m = y.shape[0]
    seq = m // batch
    tb = chunk * chunks_per_step
    nt = seq // tb
    has_init = conv0 is not None
    row = lambda b, t: b * nt + t
    in_specs = [
        pl.BlockSpec((tb, DN_CONV_CH), lambda b, t: (row(b, t), COL_CONV // DN_CONV_CH)),
        pl.BlockSpec((tb, DN_V), lambda b, t: (row(b, t), COL_Z // DN_V)),
        pl.BlockSpec((tb, BD_WIDTH), lambda b, t: (row(b, t), y.shape[1] // BD_WIDTH - 1 - _BD_TAIL_BLOCKS)),
        pl.BlockSpec((None, DN_CONV, DN_CONV_CH), lambda b, t: (layer, 0, 0)),
        pl.BlockSpec((None, 1, LANES), lambda b, t: (layer, 0, 0)),
        pl.BlockSpec((None, 1, LANES), lambda b, t: (layer, 0, 0)),
        pl.BlockSpec((None, 1, DN_DV), lambda b, t: (layer, 0, 0)),
    ]
    args = [y, y, y, conv_w, alog_row, dt_row, out_norm]
    if has_init:
        in_specs += [
            pl.BlockSpec((None, None, DN_CONV - 1, DN_CONV_CH), lambda b, t: (layer, b, 0, 0)),
            pl.BlockSpec((None, None, DN_HEADS, DN_DK, DN_DV), lambda b, t: (layer, b, 0, 0, 0)),
        ]
        args += [conv0, s0]
    return pl.pallas_call(
        functools.partial(_gdn_kernel, chunk=chunk, chunks_per_step=chunks_per_step, has_init=has_init),
        grid=(batch, nt),
        in_specs=in_specs,
        out_specs=[
            pl.BlockSpec((tb, DN_V), lambda b, t: (row(b, t), 0)),
            pl.BlockSpec((None, DN_HEADS, DN_DK, DN_DV), lambda b, t: (b, 0, 0, 0)),
            pl.BlockSpec((None, DN_CONV - 1, DN_CONV_CH), lambda b, t: (b, 0, 0)),
        ],
        out_shape=[
            jax.ShapeDtypeStruct((m, DN_V), out_dtype),
            jax.ShapeDtypeStruct((batch, DN_HEADS, DN_DK, DN_DV), F32),
            jax.ShapeDtypeStruct((batch, DN_CONV - 1, DN_CONV_CH), F32),
        ],
        scratch_shapes=[pltpu.VMEM((tb + SUBLANES, DN_CONV_CH), F32)],
        compiler_params=_params("arbitrary", "arbitrary"),
        name="gdn",
    )(*args)


def _seg_rms_scale(x, heads, mean=True):
    w = x.shape[1]
    hd = w // heads
    shift = int(math.log2(hd))
    norm = 1.0 / hd if mean else 1.0
    seg = (jnp.right_shift(lax.broadcasted_iota(jnp.int32, (w, LANES), 0), shift)
           == lax.broadcasted_iota(jnp.int32, (w, LANES), 1)).astype(BF16)
    seg_t = (lax.broadcasted_iota(jnp.int32, (LANES, w), 0)
             == jnp.right_shift(lax.broadcasted_iota(jnp.int32, (LANES, w), 1), shift)).astype(BF16)
    x2 = x * x
    x2_hi = x2.astype(BF16)
    x2_lo = (x2 - x2_hi.astype(F32)).astype(BF16)
    scale = lax.rsqrt((_dot(x2_hi, seg) + _dot(x2_lo, seg)) * norm + NORM_EPS)
    hi = scale.astype(BF16)
    lo = (scale - hi.astype(F32)).astype(BF16)
    return _dot(hi, seg_t) + _dot(lo, seg_t)


def _same_shape_fold(xs, op):
    acc = {}
    for x in xs:
        acc[x.shape] = x if x.shape not in acc else op(acc[x.shape], x)
    return list(acc.values())


def _attn_groups(groups):
    tq = groups[0][0][0].shape[0]
    qg = [jnp.concatenate(g[0], axis=0).astype(BF16) for g in groups]
    raw = [[_dot_nt(q, k) for k in g[1]] for q, g in zip(qg, groups)]
    probs, sink_terms = [], []
    for r, g in zip(raw, groups):
        sink = g[4]
        ss = [s + b for s, b in zip(r, g[3])]
        m = sink
        for s in _same_shape_fold(ss, jnp.maximum):
            m = jnp.maximum(m, jnp.max(s, axis=-1, keepdims=True))
        probs.append([jnp.exp(s - m).astype(BF16) for s in ss])
        sink_terms.append(jnp.exp(sink - m))
    pv = [[_dot(p, v) for p, v in zip(ps, g[2])] for ps, g in zip(probs, groups)]
    rowsum = [[_dot(p, jnp.ones((p.shape[1], LANES), BF16)) for p in ps] for ps in probs]
    outs = []
    for parts, sums, es in zip(pv, rowsum, sink_terms):
        o = parts[0]
        for part in parts[1:]:
            o = o + part
        den = sums[0]
        for part in sums[1:]:
            den = den + part
        o = o * (1.0 / (den + es))[:, :o.shape[1]]
        outs += [o[tq * j:tq * (j + 1), :] for j in range(SW_GROUP)]
    return outs


def _head_cols(x, heads):
    return [x[:, SW_HD * h:SW_HD * (h + 1)] for h in heads]


def _swa_prompt_kernel(q_ref, kp_ref, kc_ref, vp_ref, vc_ref, kq_ref, kn_ref, sink_ref, bias_ref,
                       o_ref, kout_ref, vout_ref, *, qb):
    n = pl.program_id(1)
    nb = pl.num_programs(1)
    kv_heads = range(SW_KV_HEADS)
    q = q_ref[...]
    qn = q * _seg_rms_scale(q, SW_HEADS)
    kc = kc_ref[...]
    kc = kc * _seg_rms_scale(kc, SW_KV_HEADS) * kn_ref[...]
    kp = kp_ref[...]
    kp = kp * _seg_rms_scale(kp, SW_KV_HEADS) * kn_ref[...]
    k_all = jnp.concatenate([kp, kc], axis=0) * kq_ref[...]
    vc = vc_ref[...]
    v_all = jnp.concatenate([vp_ref[...], vc], axis=0)
    first = n == 0
    groups = []
    for s in range(qb):
        blk = lambda a, i: a[SW_BLOCK * i:SW_BLOCK * (i + 1), :]
        qs = blk(qn, s)
        ks = [_head_cols(blk(k_all, s + i), kv_heads) for i in range(2)]
        vs = [_head_cols(blk(v_all, s + i), kv_heads) for i in range(2)]
        for g in kv_heads:
            bias = bias_ref[g]
            bias_prev = bias[:, :SW_BLOCK]
            if s == 0:
                bias_prev = jnp.where(first, NEG_INF, bias_prev)
            groups.append((
                _head_cols(qs, range(SW_GROUP * g, SW_GROUP * (g + 1))),
                [ks[0][g].astype(BF16), ks[1][g].astype(BF16)],
                [vs[0][g].astype(BF16), vs[1][g].astype(BF16)],
                [bias_prev, bias[:, SW_BLOCK:]],
                sink_ref[g]))
    outs = _attn_groups(groups)
    for s in range(qb):
        o_ref[SW_BLOCK * s:SW_BLOCK * (s + 1), :] = jnp.concatenate(
            outs[SW_HEADS * s:SW_HEADS * (s + 1)], axis=-1).astype(o_ref.dtype)

    @pl.when(n == nb - 1)
    def _():
        kout_ref[...] = kc[SW_BLOCK * (qb - 1):, :]
        vout_ref[...] = vc[SW_BLOCK * (qb - 1):, :]


def _swa_prompt(y, batch, layer, q_norm, k_norm, sink_col, bias, col_q, out_dtype, qb):
    m = y.shape[0]
    seq = m // batch
    tq = SW_BLOCK * qb
    nb = seq // tq
    cq = col_q // SW_Q
    ck = (col_q + SW_Q) // SW_KV
    cv = ck + 1
    cur = lambda b, n: b * nb + n
    prev = lambda b, n: (b * nb + n) * qb - jnp.minimum(n, 1)
    return pl.pallas_call(
        functools.partial(_swa_prompt_kernel, qb=qb),
        grid=(batch, nb),
        in_specs=[
            pl.BlockSpec((tq, SW_Q), lambda b, n: (cur(b, n), cq)),
            pl.BlockSpec((SW_BLOCK, SW_KV), lambda b, n: (prev(b, n), ck)),
            pl.BlockSpec((tq, SW_KV), lambda b, n: (cur(b, n), ck)),
            pl.BlockSpec((SW_BLOCK, SW_KV), lambda b, n: (prev(b, n), cv)),
            pl.BlockSpec((tq, SW_KV), lambda b, n: (cur(b, n), cv)),
            pl.BlockSpec((None, 1, SW_KV), lambda b, n: (layer, 0, 0)),
            pl.BlockSpec((None, 1, SW_KV), lambda b, n: (layer, 0, 0)),
            pl.BlockSpec((None, SW_KV_HEADS, SW_GROUP * SW_BLOCK, 1), lambda b, n: (layer, 0, 0, 0)),
            pl.BlockSpec((SW_KV_HEADS, SW_GROUP * SW_BLOCK, 2 * SW_BLOCK), lambda b, n: (0, 0, 0)),
        ],
        out_specs=[
            pl.BlockSpec((tq, SW_Q), lambda b, n: (cur(b, n), 0)),
            pl.BlockSpec((None, WINDOW, SW_KV), lambda b, n: (b, 0, 0)),
            pl.BlockSpec((None, WINDOW, SW_KV), lambda b, n: (b, 0, 0)),
        ],
        out_shape=[
            jax.ShapeDtypeStruct((m, SW_Q), out_dtype),
            jax.ShapeDtypeStruct((batch, WINDOW, SW_KV), F32),
            jax.ShapeDtypeStruct((batch, WINDOW, SW_KV), F32),
        ],
        compiler_params=_params("arbitrary", "arbitrary"),
        name="swa_prompt",
    )(y, y, y, y, y, q_norm, k_norm, sink_col, bias)


def _swa_sample_kernel(q_ref, k_ref, v_ref, kbuf_ref, vbuf_ref, kq_ref, kn_ref, sink_ref, bb_ref, bn_ref,
                       o_ref, kout_ref, vout_ref, *, bs):
    t = q_ref.shape[1]
    wc = kbuf_ref.shape[1]
    kv_heads = range(SW_KV_HEADS)
    groups = []
    for i in range(bs):
        q = q_ref[i]
        qn = q * _seg_rms_scale(q, SW_HEADS)
        kn = k_ref[i]
        kn = kn * _seg_rms_scale(kn, SW_KV_HEADS) * kn_ref[...]
        v = v_ref[i]
        kbuf = kbuf_ref[i]
        vbuf = vbuf_ref[i]
        kbs = _head_cols(kbuf * kq_ref[...], kv_heads)
        kns = _head_cols(kn * kq_ref[...], kv_heads)
        vbs = _head_cols(vbuf, kv_heads)
        vns = _head_cols(v, kv_heads)
        for g in kv_heads:
            groups.append((
                _head_cols(qn, range(SW_GROUP * g, SW_GROUP * (g + 1))),
                [kbs[g].astype(BF16), kns[g].astype(BF16)],
                [vbs[g].astype(BF16), vns[g].astype(BF16)],
                [bb_ref[g], bn_ref[g]],
                sink_ref[g]))
        kout_ref[i, 0:wc - t, :] = kbuf[t:, :]
        kout_ref[i, wc - t:wc, :] = kn
        vout_ref[i, 0:wc - t, :] = vbuf[t:, :]
        vout_ref[i, wc - t:wc, :] = v
    outs = _attn_groups(groups)
    for i in range(bs):
        o_ref[i] = jnp.concatenate(outs[SW_HEADS * i:SW_HEADS * (i + 1)], axis=-1).astype(o_ref.dtype)


def _swa_sample(y3, layer, k_buf, v_buf, q_norm, k_norm, sink_col, bias_buf, bias_new, col_q, bs):
    batch, t, _ = y3.shape
    wc = k_buf.shape[2]
    cq = col_q // SW_Q
    ck = (col_q + SW_Q) // SW_KV
    cv = ck + 1
    return pl.pallas_call(
        functools.partial(_swa_sample_kernel, bs=bs),
        grid=(batch // bs,),
        in_specs=[
            pl.BlockSpec((bs, t, SW_Q), lambda b: (b, 0, cq)),
            pl.BlockSpec((bs, t, SW_KV), lambda b: (b, 0, ck)),
            pl.BlockSpec((bs, t, SW_KV), lambda b: (b, 0, cv)),
            pl.BlockSpec((None, bs, wc, SW_KV), lambda b: (layer, b, 0, 0)),
            pl.BlockSpec((None, bs, wc, SW_KV), lambda b: (layer, b, 0, 0)),
            pl.BlockSpec((None, 1, SW_KV), lambda b: (layer, 0, 0)),
            pl.BlockSpec((None, 1, SW_KV), lambda b: (layer, 0, 0)),
            pl.BlockSpec((None, SW_KV_HEADS, SW_GROUP * t, 1), lambda b: (layer, 0, 0, 0)),
            pl.BlockSpec((SW_KV_HEADS, SW_GROUP * t, wc), lambda b: (0, 0, 0)),
            pl.BlockSpec((SW_KV_HEADS, SW_GROUP * t, t), lambda b: (0, 0, 0)),
        ],
        out_specs=[
            pl.BlockSpec((bs, t, SW_Q), lambda b: (b, 0, 0)),
            pl.BlockSpec((bs, wc, SW_KV), lambda b: (b, 0, 0)),
            pl.BlockSpec((bs, wc, SW_KV), lambda b: (b, 0, 0)),
        ],
        out_shape=[
            jax.ShapeDtypeStruct((batch, t, SW_Q), F32),
            jax.ShapeDtypeStruct((batch, wc, SW_KV), F32),
            jax.ShapeDtypeStruct((batch, wc, SW_KV), F32),
        ],
        compiler_params=_params("arbitrary"),
        name="swa_sample",
    )(y3, y3, y3, k_buf, v_buf, q_norm, k_norm, sink_col, bias_buf, bias_new)


def _mixout_kernel(x_ref, oa_ref, ob_ref, ga_ref, gb_ref, g_ref, wdn_ref, wsw_ref, wo_ref, o_ref):
    ya = _dot(oa_ref[...].astype(BF16), wdn_ref[...])
    yb = _dot(ob_ref[...].astype(BF16), wsw_ref[...])
    merged = _sigmoid(ga_ref[...]) * ya + _sigmoid(gb_ref[...]) * yb
    o_ref[...] = x_ref[...] + g_ref[...] * _dot(merged.astype(BF16), wo_ref[...])


def _mixout(x, o_a, o_b, y, mod, layer, w_dn, w_sw, w_o, tm):
    m, d = x.shape
    tiles_per_row = (m // mod.shape[1]) // tm
    cga = COL_GA // d
    const = dict(pipeline_mode=pl.Buffered(1))
    return pl.pallas_call(
        _mixout_kernel,
        grid=(m // tm,),
        in_specs=[
            pl.BlockSpec((tm, d), lambda i: (i, 0)),
            pl.BlockSpec((tm, DN_V), lambda i: (i, 0)),
            pl.BlockSpec((tm, SW_Q), lambda i: (i, 0)),
            pl.BlockSpec((tm, d), lambda i: (i, cga)),
            pl.BlockSpec((tm, d), lambda i: (i, cga + 1)),
            pl.BlockSpec((None, None, mod.shape[2], d), lambda i: (layer, i // tiles_per_row, 0, 5)),
            pl.BlockSpec((None, DN_V, d), lambda i: (layer, 0, 0), **const),
            pl.BlockSpec((None, SW_Q, d), lambda i: (layer, 0, 0), **const),
            pl.BlockSpec((None, d, d), lambda i: (layer, 0, 0), **const),
        ],
        out_specs=pl.BlockSpec((tm, d), lambda i: (i, 0)),
        out_shape=jax.ShapeDtypeStruct((m, d), F32),
        compiler_params=_params("arbitrary"),
        name="mixout",
    )(x, o_a, o_b, y, y, mod, w_dn, w_sw, w_o)


_BD_TAIL_BLOCKS = 3
TM_PROMPT = 1024
TF = 256
TF_SAMPLE = 512
TN = 1024
TM_MIX = 256
GDN_CHUNKS_PER_STEP = 4
SWA_Q_BLOCKS_PER_STEP = 2
SAMPLE_BATCH_PER_STEP = 4


def _reorder_kernel(w_ref, o_ref, *, segments):
    row = 0
    for start, height in segments:
        o_ref[row:row + height, :] = w_ref[start:start + height, :].astype(o_ref.dtype)
        row += height
    o_ref[row:, :] = jnp.zeros((o_ref.shape[0] - row, o_ref.shape[1]), o_ref.dtype)


def _reorder_w_in(w_in, d, tk=256):
    sizes = (DN_CONV_CH, DN_V, DN_HEADS, DN_HEADS, SW_Q, SW_KV, SW_KV, d, d)
    o = [int(v) for v in np.concatenate([[0], np.cumsum(sizes)])]
    segments = ((0, o[2]), (o[7], 2 * d), (o[4], o[7] - o[4]), (o[2], o[4] - o[2]))
    w_t = jnp.swapaxes(w_in, 1, 2)
    depth, n_in, rows = w_t.shape
    n_out = sum(h for _, h in segments) + BD_WIDTH - 2 * DN_HEADS + _BD_TAIL_BLOCKS * LANES
    return pl.pallas_call(
        functools.partial(_reorder_kernel, segments=segments),
        grid=(depth, rows // tk),
        in_specs=[pl.BlockSpec((None, n_in, tk), lambda l, i: (l, 0, i))],
        out_specs=pl.BlockSpec((None, n_out, tk), lambda l, i: (l, 0, i)),
        out_shape=jax.ShapeDtypeStruct((depth, n_out, rows), BF16),
        compiler_params=_params("arbitrary", "arbitrary"),
        name="reorder_w_in",
    )(w_t)


def kernel(x_prompt, x_sample, c_prompt, c_sample, state_delta, state_conv, cache_swa_k, cache_swa_v, rel_bias, w_ada, b_ada, norm_ffn1, w_ffn1_in, w_ffn1_out, norm_mix, w_in, dn_conv_w, dn_a_log, dn_dt_bias, dn_out_norm, w_dn_out, sw_q_norm, sw_k_norm, sw_sinks, w_sw_out, w_o, norm_ffn2, w_ffn2_in, w_ffn2_out):
    bp, seq, d = x_prompt.shape
    bd, dseq, _ = x_sample.shape
    depth = w_ada.shape[0]
    wc = cache_swa_k.shape[2]
    col_q = COL_GA + 2 * d

    n_rows = bp + bd
    pad_rows = -n_rows % SUBLANES
    c_all = jnp.concatenate([c_prompt, c_sample, jnp.zeros((pad_rows, d), F32)], axis=0)
    mod_all = _ada(c_all, w_ada, b_ada)
    mod_p = mod_all[:, :bp].reshape(depth, bp, 1, N_MOD * d)
    mod_s = jnp.repeat(mod_all[:, bp:n_rows], dseq, axis=1).reshape(depth, 1, bd * dseq, N_MOD * d)

    qi = np.arange(SW_BLOCK)[:, None]
    kj = np.arange(2 * SW_BLOCK)[None, :]
    bias_p = _bias_grid(rel_bias, _bucket_map(qi + SW_BLOCK - kj))
    ti = np.arange(dseq)[:, None]
    sj = np.arange(wc + dseq)[None, :]
    dist_s = ti + wc - sj
    bias_sb = _bias_grid(rel_bias, _bucket_map(dist_s[:, :wc]))
    bias_sn = _bias_grid(rel_bias, _bucket_map(dist_s[:, wc:]))

    w_in_r = _reorder_w_in(w_in, d)
    w_dn16 = w_dn_out.astype(BF16)
    w_sw16 = w_sw_out.astype(BF16)
    w_o16 = w_o.astype(BF16)
    r3 = lambda a: a.reshape(depth, 1, a.shape[-1])
    lane_row = lambda a: jnp.pad(a, ((0, 0), (DN_HEADS, LANES - 2 * DN_HEADS))).reshape(depth, 1, LANES)
    alog_row = lane_row(dn_a_log)
    dt_row = lane_row(dn_dt_bias)
    sinks = sw_sinks.reshape(depth, SW_KV_HEADS, SW_GROUP, 1)
    sink_p = jnp.repeat(sinks, SW_BLOCK, axis=2)
    sink_s = jnp.repeat(sinks, dseq, axis=2)
    kbuf = cache_swa_k.reshape(depth, bd, wc, SW_KV)
    vbuf = cache_swa_v.reshape(depth, bd, wc, SW_KV)
    g_ffn1, g_mix, g_ffn2, g_on = r3(norm_ffn1), r3(norm_mix), r3(norm_ffn2), r3(dn_out_norm)
    g_qn = r3(jnp.tile(sw_q_norm * SW_SCALE, (1, SW_KV_HEADS)))
    g_kn = r3(jnp.tile(sw_k_norm, (1, SW_KV_HEADS)))

    xp = x_prompt.reshape(bp * seq, d)
    xs = x_sample.reshape(bd * dseq, d)
    ms = bd * dseq
    outs = [[] for _ in range(8)]
    for l in range(depth):
        xp = _ffn(xp, mod_p, 0, l, g_ffn1, w_ffn1_in, w_ffn1_out, TM_PROMPT, TF)
        yp = _inproj(xp, mod_p, l, g_mix, w_in_r, TM_PROMPT, TN)
        oa, s_p, conv_p = _gdn(yp, bp, l, dn_conv_w, alog_row, dt_row, g_on, None, None, DN_CHUNK,
                               GDN_CHUNKS_PER_STEP, BF16)
        ob, k_p, v_p = _swa_prompt(yp, bp, l, g_qn, g_kn, sink_p, bias_p, col_q, BF16, SWA_Q_BLOCKS_PER_STEP)
        xp = _mixout(xp, oa, ob, yp, mod_p, l, w_dn16, w_sw16, w_o16, TM_MIX)
        xp = _ffn(xp, mod_p, 6, l, g_ffn2, w_ffn2_in, w_ffn2_out, TM_PROMPT, TF)
        xs = _ffn(xs, mod_s, 0, l, g_ffn1, w_ffn1_in, w_ffn1_out, ms, TF_SAMPLE)
        ys = _inproj(xs, mod_s, l, g_mix, w_in_r, ms, TN)
        oa, s_s, conv_s = _gdn(ys, bd, l, dn_conv_w, alog_row, dt_row, g_on, state_conv, state_delta, dseq, 1, F32)
        ob, k_s, v_s = _swa_sample(ys.reshape(bd, dseq, -1), l, kbuf, vbuf, g_qn, g_kn, sink_s, bias_sb, bias_sn,
                                   col_q, SAMPLE_BATCH_PER_STEP)
        xs = _mixout(xs, oa, ob.reshape(ms, SW_Q), ys, mod_s, l, w_dn16, w_sw16, w_o16, ms)
        xs = _ffn(xs, mod_s, 6, l, g_ffn2, w_ffn2_in, w_ffn2_out, ms, TF_SAMPLE)
        for lst, val in zip(outs, (s_p, conv_p, k_p, v_p, s_s, conv_s, k_s, v_s)):
            lst.append(val)
    s_p, conv_p, k_p, v_p, s_s, conv_s, k_s, v_s = (jnp.stack(o) for o in outs)
    kv = lambda a: a.reshape(a.shape[:3] + (SW_KV_HEADS, SW_HD))
    return (xp.reshape(bp, seq, d), xs.reshape(bd, dseq, d), s_p, conv_p, kv(k_p), kv(v_p),
            s_s, conv_s, kv(k_s), kv(v_s))
```

```python
import functools
import math

import numpy as np
import jax
import jax.numpy as jnp
from jax import lax
from jax.experimental import pallas as pl
from jax.experimental.pallas import tpu as pltpu

F32 = jnp.float32
BF16 = jnp.bfloat16

NORM_EPS = 1e-6
NEG_INF = -1e30
N_MOD = 9

DN_HEADS = 8
DN_DK = 128
DN_DV = 128
DN_CONV = 4
DN_CHUNK = 64
DN_QK = DN_HEADS * DN_DK
DN_V = DN_HEADS * DN_DV
DN_CONV_CH = 2 * DN_QK + DN_V
DN_SCALE = DN_DK ** -0.5

SW_HEADS = 16
SW_KV_HEADS = 4
SW_HD = 64
SW_GROUP = SW_HEADS // SW_KV_HEADS
SW_SCALE = SW_HD ** -0.5
WINDOW = 128
SW_BLOCK = 128
SW_Q = SW_HEADS * SW_HD
SW_KV = SW_KV_HEADS * SW_HD

N_BUCKETS = 32
MAX_DISTANCE = 128

LANES = 128
SUBLANES = 8
VMEM_LIMIT_BYTES = 56 * 1024 * 1024
LARGE_TILE_VMEM_LIMIT_BYTES = 60 * 1024 * 1024

COL_CONV = 0
COL_Z = COL_CONV + DN_CONV_CH
COL_GA = COL_Z + DN_V
BD_WIDTH = LANES


def _sigmoid(x):
    return 1.0 / (1.0 + jnp.exp(-x))


def _silu(x):
    return x * _sigmoid(x)


def _softplus(x):
    return jnp.maximum(x, 0.0) + jnp.log1p(jnp.exp(-jnp.abs(x)))


def _dot(a, b, **kw):
    return jnp.dot(a, b, preferred_element_type=F32, **kw)


def _dot_nt(a, b):
    return lax.dot_general(a, b, (((1,), (1,)), ((), ())), preferred_element_type=F32)


def _dot_tn(a, b):
    return lax.dot_general(a, b, (((0,), (0,)), ((), ())), preferred_element_type=F32)


def _params(*semantics, vmem_limit_bytes=VMEM_LIMIT_BYTES):
    return pltpu.CompilerParams(dimension_semantics=semantics, vmem_limit_bytes=vmem_limit_bytes)


def _mod_norm(x, gain, sc, sh):
    r = lax.rsqrt(jnp.mean(x * x, axis=-1, keepdims=True) + NORM_EPS)
    if sc.shape[0] == 1:
        return (x * r) * (gain * (1.0 + sc)) + sh
    return x * r * gain * (1.0 + sc) + sh


def _ada_kernel(c_ref, w_ref, b_ref, o_ref):
    a = _silu(c_ref[...]).astype(BF16)
    o_ref[...] = _dot(a, w_ref[...].astype(BF16)) + b_ref[...]


def _ada(c_all, w_ada, b_ada, tn=1024):
    depth, d, n = w_ada.shape
    rows = c_all.shape[0]
    return pl.pallas_call(
        _ada_kernel,
        grid=(depth, n // tn),
        in_specs=[
            pl.BlockSpec((rows, d), lambda l, j: (0, 0)),
            pl.BlockSpec((None, d, tn), lambda l, j: (l, 0, j)),
            pl.BlockSpec((None, 1, tn), lambda l, j: (l, 0, j)),
        ],
        out_specs=pl.BlockSpec((None, rows, tn), lambda l, j: (l, 0, j)),
        out_shape=jax.ShapeDtypeStruct((depth, rows, n), F32),
        compiler_params=_params("arbitrary", "arbitrary"),
        name="ada",
    )(c_all, w_ada, b_ada.reshape(depth, 1, n))


def _bucket_table():
    max_exact = N_BUCKETS // 2
    d = np.arange(WINDOW + 1)
    ratio = np.log(np.maximum(d, 1).astype(np.float32) / np.float32(max_exact)) / np.float32(math.log(MAX_DISTANCE / max_exact))
    large = max_exact + (ratio.astype(np.float32) * np.float32(N_BUCKETS - max_exact)).astype(np.int32)
    return np.where(d < max_exact, d, np.minimum(large, N_BUCKETS - 1)).astype(np.int32)


def _bucket_map(dist):
    table = _bucket_table()
    valid = (dist >= 0) & (dist <= WINDOW)
    return np.where(valid, table[np.clip(dist, 0, WINDOW)], -1).astype(np.int32)


def _bias_kernel(rb_ref, bm_ref, o_ref):
    bm = bm_ref[...]
    tq = bm.shape[0]
    for g in range(SW_KV_HEADS):
        for j in range(SW_GROUP):
            acc = jnp.full(bm.shape, NEG_INF, F32)
            for b in range(N_BUCKETS):
                acc = jnp.where(bm == b, rb_ref[b, g * SW_GROUP + j], acc)
            o_ref[g, j * tq:(j + 1) * tq, :] = acc


def _bias_grid(rel_bias, bmap):
    tq, tk = bmap.shape
    return pl.pallas_call(
        _bias_kernel,
        in_specs=[
            pl.BlockSpec(memory_space=pltpu.SMEM),
            pl.BlockSpec((tq, tk), lambda: (0, 0)),
        ],
        out_specs=pl.BlockSpec((SW_KV_HEADS, SW_GROUP * tq, tk), lambda: (0, 0, 0)),
        out_shape=jax.ShapeDtypeStruct((SW_KV_HEADS, SW_GROUP * tq, tk), F32),
        name="relbias",
    )(rel_bias, jnp.asarray(bmap))


def _ffn_kernel(x_ref, sh_ref, sc_ref, g_ref, gain_ref, wg_ref, wu_ref, wo_ref, o_ref, h_ref, *, nf):
    f = pl.program_id(1)

    @pl.when(f == 0)
    def _():
        h_ref[...] = _mod_norm(x_ref[...], gain_ref[...], sc_ref[...], sh_ref[...]).astype(BF16)
        o_ref[...] = jnp.zeros_like(o_ref)

    h = h_ref[...]
    gate = _dot(h, wg_ref[...].astype(BF16))
    up = _dot(h, wu_ref[...].astype(BF16))
    a = (_silu(gate) * up).astype(BF16)
    o_ref[...] += _dot(a, wo_ref[...].astype(BF16))

    @pl.when(f == nf - 1)
    def _():
        o_ref[...] = x_ref[...] + (0.5 * g_ref[...]) * o_ref[...]


def _mod_spec(mod, layer, seg, d, tiles_per_row):
    r = mod.shape[2]
    return pl.BlockSpec((None, None, r, d), lambda i, j: (layer, i // tiles_per_row, 0, seg))


def _ffn(x, mod, seg0, layer, gain, w_in, w_out, tm, tf):
    m, d = x.shape
    ff = w_out.shape[1]
    nf = ff // tf
    tiles_per_row = (m // mod.shape[1]) // tm
    return pl.pallas_call(
        functools.partial(_ffn_kernel, nf=nf),
        grid=(m // tm, nf),
        in_specs=[
            pl.BlockSpec((tm, d), lambda i, f: (i, 0)),
            _mod_spec(mod, layer, seg0, d, tiles_per_row),
            _mod_spec(mod, layer, seg0 + 1, d, tiles_per_row),
            _mod_spec(mod, layer, seg0 + 2, d, tiles_per_row),
            pl.BlockSpec((None, 1, d), lambda i, f: (layer, 0, 0)),
            pl.BlockSpec((None, d, tf), lambda i, f: (layer, 0, f)),
            pl.BlockSpec((None, d, tf), lambda i, f: (layer, 0, f + nf)),
            pl.BlockSpec((None, tf, d), lambda i, f: (layer, f, 0)),
        ],
        out_specs=pl.BlockSpec((tm, d), lambda i, f: (i, 0)),
        out_shape=jax.ShapeDtypeStruct((m, d), F32),
        scratch_shapes=[pltpu.VMEM((tm, d), BF16)],
        compiler_params=_params("arbitrary", "arbitrary", vmem_limit_bytes=LARGE_TILE_VMEM_LIMIT_BYTES),
        name="ffn",
    )(x, mod, mod, mod, gain, w_in, w_in, w_out)


def _inproj_kernel(x_ref, sh_ref, sc_ref, gain_ref, w_ref, o_ref, h_ref):
    @pl.when(pl.program_id(1) == 0)
    def _():
        h_ref[...] = _mod_norm(x_ref[...], gain_ref[...], sc_ref[...], sh_ref[...]).astype(BF16)

    o_ref[...] = _dot_nt(h_ref[...], w_ref[...])


def _inproj(x, mod, layer, gain, w_r, tm, tn):
    m, d = x.shape
    nr = w_r.shape[1]
    tiles_per_row = (m // mod.shape[1]) // tm
    return pl.pallas_call(
        _inproj_kernel,
        grid=(m // tm, nr // tn),
        in_specs=[
            pl.BlockSpec((tm, d), lambda i, j: (i, 0)),
            _mod_spec(mod, layer, 3, d, tiles_per_row),
            _mod_spec(mod, layer, 4, d, tiles_per_row),
            pl.BlockSpec((None, 1, d), lambda i, j: (layer, 0, 0)),
            pl.BlockSpec((None, tn, d), lambda i, j: (layer, j, 0)),
        ],
        out_specs=pl.BlockSpec((tm, tn), lambda i, j: (i, j)),
        out_shape=jax.ShapeDtypeStruct((m, nr), F32),
        scratch_shapes=[pltpu.VMEM((tm, d), BF16)],
        compiler_params=_params("arbitrary", "arbitrary"),
        name="inproj",
    )(x, mod, mod, gain, w_r)


def _unit_lower_inverses(lmats, ri, ci):
    c = lmats[0].shape[0]
    n = len(lmats)
    eye = (ri == ci).astype(F32)
    if c == SUBLANES:
        ts = [eye for _ in lmats]
        for j in range(c - 1):
            ts = [t - l[:, j:j + 1] * t[j:j + 1, :] for t, l in zip(ts, lmats)]
        return ts
    pair = (jnp.right_shift(ri, 1) == jnp.right_shift(ci, 1)) & (ri == ci + 1)
    ts = [eye - jnp.where(pair, l, 0.0) for l in lmats]
    shift = 1
    while (1 << shift) < c:
        rb = jnp.right_shift(ri, shift)
        cb = jnp.right_shift(ci, shift)
        off = (jnp.right_shift(rb, 1) == jnp.right_shift(cb, 1)) & (rb == cb + 1)
        t16 = [t.astype(BF16) for t in ts]
        ps = [_dot(t16[h], jnp.where(off, lmats[h], 0.0).astype(BF16)).astype(BF16) for h in range(n)]
        xs = [_dot(ps[h], t16[h]) for h in range(n)]
        ts = [ts[h] - xs[h] for h in range(n)]
        shift += 1
    return ts


def _gdn_kernel(*refs, chunk, chunks_per_step, seqs, has_init):
    if has_init:
        (xc_ref, z_ref, bd_ref, cw_ref, alog_ref, dt_ref, on_ref, conv0_ref, s0_ref,
         o_ref, s_ref, conv_ref, xs_ref) = refs
    else:
        (xc_ref, z_ref, bd_ref, cw_ref, alog_ref, dt_ref, on_ref,
         o_ref, s_ref, conv_ref, xs_ref) = refs
    t = pl.program_id(1)
    nt = pl.num_programs(1)
    c = chunk
    tb = c * chunks_per_step
    hist = SUBLANES
    stride = hist + tb
    seq_ids = range(seqs)

    @pl.when(t == 0)
    def _():
        for s in seq_ids:
            xs_ref[stride * s:stride * s + hist, :] = jnp.zeros((hist, DN_CONV_CH), F32)
            if has_init:
                xs_ref[stride * s + hist - (DN_CONV - 1):stride * s + hist, :] = conv0_ref[s]
        if has_init:
            s_ref[...] = s0_ref[...]
        else:
            s_ref[...] = jnp.zeros_like(s_ref)

    accs = []
    for s in seq_ids:
        base = stride * s + hist
        xs_ref[base:base + tb, :] = xc_ref[tb * s:tb * (s + 1), :]
        acc = xs_ref[base:base + tb, :] * cw_ref[DN_CONV - 1:DN_CONV, :]
        for i in range(DN_CONV - 1):
            off = base - (DN_CONV - 1) + i
            acc = acc + xs_ref[off:off + tb, :] * cw_ref[i:i + 1, :]
        accs.append(acc)
    xc_all = _silu(accs[0] if seqs == 1 else jnp.concatenate(accs, axis=0))

    @pl.when(t == nt - 1)
    def _():
        for s in seq_ids:
            conv_ref[s] = xs_ref[stride * (s + 1) - (DN_CONV - 1):stride * (s + 1), :]

    for s in seq_ids:
        xs_ref[stride * s:stride * s + hist, :] = xs_ref[stride * s + tb:stride * s + tb + hist, :]

    bd = bd_ref[...]
    beta_blk = _sigmoid(bd)
    g_blk = -jnp.exp(alog_ref[...]) * _softplus(bd + dt_ref[...])
    ri = lax.broadcasted_iota(jnp.int32, (c, c), 0)
    ci = lax.broadcasted_iota(jnp.int32, (c, c), 1)
    incl = ri >= ci
    strict = ri > ci
    tri = incl.astype(F32)

    on = on_ref[...]
    heads = range(DN_HEADS)
    blocks = [(s, j) for s in seq_ids for j in range(chunks_per_step)]
    q_all = xc_all[:, 0:DN_QK]
    k_all = xc_all[:, DN_QK:2 * DN_QK]
    rows_all = seqs * tb
    if tb >= DN_CHUNK:
        q_scale = _seg_rms_scale(q_all, DN_HEADS, mean=False)
        k_scale = _seg_rms_scale(k_all, DN_HEADS, mean=False)
    else:
        q_scale, k_scale = (jnp.concatenate(
            [jnp.broadcast_to(lax.rsqrt(jnp.sum(jnp.square(a[:, DN_DK * h:DN_DK * (h + 1)]), axis=-1, keepdims=True)
                                        + NORM_EPS), (rows_all, DN_DK)) for h in heads], axis=-1) for a in (q_all, k_all))
    qn_all = q_all * (q_scale * DN_SCALE)
    kn_all = k_all * k_scale
    qs, ks, vs, betas, gcols, glasts, decays, egcs = [], [], [], [], [], [], [], []
    for s, j in blocks:
        rows = slice(tb * s + c * j, tb * s + c * (j + 1))
        xc = xc_all[rows, :]
        qn = qn_all[rows, :]
        kn = kn_all[rows, :]
        beta_all = beta_blk[rows, :]
        gc_all = _dot(tri, g_blk[rows, :], precision=lax.Precision.HIGHEST)
        gc_pad = gc_all if c == LANES else jnp.concatenate([gc_all, jnp.zeros((LANES - c, LANES), F32)], axis=0)
        gc_t = gc_pad.T
        for h in heads:
            qs.append(qn[:, DN_DK * h:DN_DK * (h + 1)])
            ks.append(kn[:, DN_DK * h:DN_DK * (h + 1)])
            vs.append(xc[:, 2 * DN_QK + DN_DV * h:2 * DN_QK + DN_DV * (h + 1)])
            betas.append(beta_all[:, h:h + 1])
            gcol = gc_all[:, DN_HEADS + h:DN_HEADS + h + 1]
            grow = gc_t[DN_HEADS + h:DN_HEADS + h + 1, 0:c]
            gcols.append(gcol)
            glasts.append(gc_all[c - 1:c, DN_HEADS + h:DN_HEADS + h + 1])
            decays.append(jnp.exp(jnp.where(incl, gcol - grow, NEG_INF)))
            egcs.append(jnp.exp(gcol))
    n = len(blocks) * DN_HEADS
    kbs = [ks[i] * betas[i] for i in range(n)]
    k16 = [ks[i].astype(BF16) for i in range(n)]
    kk = [_dot_nt(kbs[i].astype(BF16), k16[i]) for i in range(n)]
    qk = [_dot_nt(qs[i].astype(BF16), k16[i]) for i in range(n)]
    lmats = [jnp.where(strict, kk[i] * decays[i], 0.0) for i in range(n)]
    tinv = _unit_lower_inverses(lmats, ri, ci)
    rhs = [jnp.concatenate([vs[i] * betas[i], kbs[i] * egcs[i]], axis=-1).astype(BF16) for i in range(n)]
    sol = [_dot(tinv[i].astype(BF16), rhs[i]) for i in range(n)]
    wq = [jnp.concatenate([sol[i][:, DN_DV:], qs[i] * egcs[i]], axis=0).astype(BF16) for i in range(n)]
    a16 = [(qk[i] * decays[i]).astype(BF16) for i in range(n)]
    kdec = [(ks[i] * jnp.exp(glasts[i] - gcols[i])).astype(BF16) for i in range(n)]
    lanes = [(s, h) for s in seq_ids for h in heads]
    state = [s_ref[s, h] for s, h in lanes]
    for j in range(chunks_per_step):
        idx = [(s * chunks_per_step + j) * DN_HEADS + h for s, h in lanes]
        m = range(len(lanes))
        s16 = [st.astype(BF16) for st in state]
        wqs = [_dot(wq[idx[a]], s16[a]) for a in m]
        v16 = [(sol[idx[a]][:, :DN_DV] - wqs[a][:c, :]).astype(BF16) for a in m]
        av = [_dot(a16[idx[a]], v16[a]) for a in m]
        kv = [_dot_tn(kdec[idx[a]], v16[a]) for a in m]
        state = [state[a] * jnp.exp(glasts[idx[a]]) + kv[a] for a in m]
        for a, (s, h) in enumerate(lanes):
            o = wqs[a][c:, :] + av[a]
            o = o * lax.rsqrt(jnp.mean(o * o, axis=-1, keepdims=True) + NORM_EPS) * on
            rows = slice(tb * s + c * j, tb * s + c * (j + 1))
            zh = z_ref[rows, DN_DV * h:DN_DV * (h + 1)]
            o_ref[rows, DN_DV * h:DN_DV * (h + 1)] = (o * _silu(zh)).astype(o_ref.dtype)
    for a, (s, h) in enumerate(lanes):
        s_ref[s, h] = state[a]


def _gdn(y, batch, layer, conv_w, alog_row, dt_row, out_norm, conv0, s0, chunk, chunks_per_step, seqs, out_dtype):
    m = y.shape[0]
    seq = m // batch
    tb = chunk * chunks_per_step
    nt = seq // tb
    assert seqs == 1 or nt == 1
    has_init = conv0 is not None
    row = lambda b, t: b * nt + t
    rb = seqs * tb
    in_specs = [
        pl.BlockSpec((rb, DN_CONV_CH), lambda b, t: (row(b, t), COL_CONV // DN_CONV_CH)),
        pl.BlockSpec((rb, DN_V), lambda b, t: (row(b, t), COL_Z // DN_V)),
        pl.BlockSpec((rb, BD_WIDTH), lambda b, t: (row(b, t), y.shape[1] // BD_WIDTH - 1 - _BD_TAIL_BLOCKS)),
        pl.BlockSpec((None, DN_CONV, DN_CONV_CH), lambda b, t: (layer, 0, 0)),
        pl.BlockSpec((None, 1, LANES), lambda b, t: (layer, 0, 0)),
        pl.BlockSpec((None, 1, LANES), lambda b, t: (layer, 0, 0)),
        pl.BlockSpec((None, 1, DN_DV), lambda b, t: (layer, 0, 0)),
    ]
    args = [y, y, y, conv_w, alog_row, dt_row, out_norm]
    if has_init:
        in_specs += [
            pl.BlockSpec((None, seqs, DN_CONV - 1, DN_CONV_CH), lambda b, t: (layer, b, 0, 0)),
            pl.BlockSpec((None, seqs, DN_HEADS, DN_DK, DN_DV), lambda b, t: (layer, b, 0, 0, 0)),
        ]
        args += [conv0, s0]
    return pl.pallas_call(
        functools.partial(_gdn_kernel, chunk=chunk, chunks_per_step=chunks_per_step, seqs=seqs, has_init=has_init),
        grid=(batch // seqs, nt),
        in_specs=in_specs,
        out_specs=[
            pl.BlockSpec((rb, DN_V), lambda b, t: (row(b, t), 0)),
            pl.BlockSpec((seqs, DN_HEADS, DN_DK, DN_DV), lambda b, t: (b, 0, 0, 0)),
            pl.BlockSpec((seqs, DN_CONV - 1, DN_CONV_CH), lambda b, t: (b, 0, 0)),
        ],
        out_shape=[
            jax.ShapeDtypeStruct((m, DN_V), out_dtype),
            jax.ShapeDtypeStruct((batch, DN_HEADS, DN_DK, DN_DV), F32),
            jax.ShapeDtypeStruct((batch, DN_CONV - 1, DN_CONV_CH), F32),
        ],
        scratch_shapes=[pltpu.VMEM((seqs * (tb + SUBLANES), DN_CONV_CH), F32)],
        compiler_params=_params("arbitrary", "arbitrary"),
        name="gdn",
    )(*args)


def _seg_rms_scale(x, heads, mean=True):
    w = x.shape[1]
    hd = w // heads
    shift = int(math.log2(hd))
    norm = 1.0 / hd if mean else 1.0
    seg = (jnp.right_shift(lax.broadcasted_iota(jnp.int32, (w, LANES), 0), shift)
           == lax.broadcasted_iota(jnp.int32, (w, LANES), 1)).astype(BF16)
    seg_t = (lax.broadcasted_iota(jnp.int32, (LANES, w), 0)
             == jnp.right_shift(lax.broadcasted_iota(jnp.int32, (LANES, w), 1), shift)).astype(BF16)
    x2 = x * x
    x2_hi = x2.astype(BF16)
    x2_lo = (x2 - x2_hi.astype(F32)).astype(BF16)
    scale = lax.rsqrt((_dot(x2_hi, seg) + _dot(x2_lo, seg)) * norm + NORM_EPS)
    hi = scale.astype(BF16)
    lo = (scale - hi.astype(F32)).astype(BF16)
    return _dot(hi, seg_t) + _dot(lo, seg_t)


def _same_shape_fold(xs, op):
    acc = {}
    for x in xs:
        acc[x.shape] = x if x.shape not in acc else op(acc[x.shape], x)
    return list(acc.values())


def _attn_groups(groups):
    tq = groups[0][0][0].shape[0]
    qg = [jnp.concatenate(g[0], axis=0).astype(BF16) for g in groups]
    raw = [[_dot_nt(q, k) for k in g[1]] for q, g in zip(qg, groups)]
    probs, sink_terms = [], []
    for r, g in zip(raw, groups):
        sink = g[4]
        ss = [s + b for s, b in zip(r, g[3])]
        m = sink
        for s in _same_shape_fold(ss, jnp.maximum):
            m = jnp.maximum(m, jnp.max(s, axis=-1, keepdims=True))
        probs.append([jnp.exp(s - m).astype(BF16) for s in ss])
        sink_terms.append(jnp.exp(sink - m))
    pv = [[_dot(p, v) for p, v in zip(ps, g[2])] for ps, g in zip(probs, groups)]
    rowsum = [[_dot(p, jnp.ones((p.shape[1], LANES), BF16)) for p in ps] for ps in probs]
    outs = []
    for parts, sums, es in zip(pv, rowsum, sink_terms):
        o = parts[0]
        for part in parts[1:]:
            o = o + part
        den = sums[0]
        for part in sums[1:]:
            den = den + part
        o = o * (1.0 / (den + es))[:, :o.shape[1]]
        outs += [o[tq * j:tq * (j + 1), :] for j in range(SW_GROUP)]
    return outs


def _head_cols(x, heads):
    return [x[:, SW_HD * h:SW_HD * (h + 1)] for h in heads]


def _swa_prompt_kernel(q_ref, kp_ref, kc_ref, vp_ref, vc_ref, kq_ref, kn_ref, sink_ref, bias_ref,
                       o_ref, kout_ref, vout_ref, *, qb):
    n = pl.program_id(1)
    nb = pl.num_programs(1)
    kv_heads = range(SW_KV_HEADS)
    q = q_ref[...]
    qn = q * _seg_rms_scale(q, SW_HEADS)
    kc = kc_ref[...]
    kc = kc * _seg_rms_scale(kc, SW_KV_HEADS) * kn_ref[...]
    kp = kp_ref[...]
    kp = kp * _seg_rms_scale(kp, SW_KV_HEADS) * kn_ref[...]
    k_all = jnp.concatenate([kp, kc], axis=0) * kq_ref[...]
    vc = vc_ref[...]
    v_all = jnp.concatenate([vp_ref[...], vc], axis=0)
    first = n == 0
    groups = []
    for s in range(qb):
        blk = lambda a, i: a[SW_BLOCK * i:SW_BLOCK * (i + 1), :]
        qs = blk(qn, s)
        ks = [_head_cols(blk(k_all, s + i), kv_heads) for i in range(2)]
        vs = [_head_cols(blk(v_all, s + i), kv_heads) for i in range(2)]
        for g in kv_heads:
            bias = bias_ref[g]
            bias_prev = bias[:, :SW_BLOCK]
            if s == 0:
                bias_prev = jnp.where(first, NEG_INF, bias_prev)
            groups.append((
                _head_cols(qs, range(SW_GROUP * g, SW_GROUP * (g + 1))),
                [ks[0][g].astype(BF16), ks[1][g].astype(BF16)],
                [vs[0][g].astype(BF16), vs[1][g].astype(BF16)],
                [bias_prev, bias[:, SW_BLOCK:]],
                sink_ref[g]))
    outs = _attn_groups(groups)
    for s in range(qb):
        o_ref[SW_BLOCK * s:SW_BLOCK * (s + 1), :] = jnp.concatenate(
            outs[SW_HEADS * s:SW_HEADS * (s + 1)], axis=-1).astype(o_ref.dtype)

    @pl.when(n == nb - 1)
    def _():
        kout_ref[...] = kc[SW_BLOCK * (qb - 1):, :]
        vout_ref[...] = vc[SW_BLOCK * (qb - 1):, :]


def _swa_prompt(y, batch, layer, q_norm, k_norm, sink_col, bias, col_q, out_dtype, qb):
    m = y.shape[0]
    seq = m // batch
    tq = SW_BLOCK * qb
    nb = seq // tq
    cq = col_q // SW_Q
    ck = (col_q + SW_Q) // SW_KV
    cv = ck + 1
    cur = lambda b, n: b * nb + n
    prev = lambda b, n: (b * nb + n) * qb - jnp.minimum(n, 1)
    return pl.pallas_call(
        functools.partial(_swa_prompt_kernel, qb=qb),
        grid=(batch, nb),
        in_specs=[
            pl.BlockSpec((tq, SW_Q), lambda b, n: (cur(b, n), cq)),
            pl.BlockSpec((SW_BLOCK, SW_KV), lambda b, n: (prev(b, n), ck)),
            pl.BlockSpec((tq, SW_KV), lambda b, n: (cur(b, n), ck)),
            pl.BlockSpec((SW_BLOCK, SW_KV), lambda b, n: (prev(b, n), cv)),
            pl.BlockSpec((tq, SW_KV), lambda b, n: (cur(b, n), cv)),
            pl.BlockSpec((None, 1, SW_KV), lambda b, n: (layer, 0, 0)),
            pl.BlockSpec((None, 1, SW_KV), lambda b, n: (layer, 0, 0)),
            pl.BlockSpec((None, SW_KV_HEADS, SW_GROUP * SW_BLOCK, 1), lambda b, n: (layer, 0, 0, 0)),
            pl.BlockSpec((SW_KV_HEADS, SW_GROUP * SW_BLOCK, 2 * SW_BLOCK), lambda b, n: (0, 0, 0)),
        ],
        out_specs=[
            pl.BlockSpec((tq, SW_Q), lambda b, n: (cur(b, n), 0)),
            pl.BlockSpec((None, WINDOW, SW_KV), lambda b, n: (b, 0, 0)),
            pl.BlockSpec((None, WINDOW, SW_KV), lambda b, n: (b, 0, 0)),
        ],
        out_shape=[
            jax.ShapeDtypeStruct((m, SW_Q), out_dtype),
            jax.ShapeDtypeStruct((batch, WINDOW, SW_KV), F32),
            jax.ShapeDtypeStruct((batch, WINDOW, SW_KV), F32),
        ],
        compiler_params=_params("arbitrary", "arbitrary"),
        name="swa_prompt",
    )(y, y, y, y, y, q_norm, k_norm, sink_col, bias)


def _swa_sample_kernel(q_ref, k_ref, v_ref, kbuf_ref, vbuf_ref, kq_ref, kn_ref, sink_ref, bb_ref, bn_ref,
                       o_ref, kout_ref, vout_ref, *, bs):
    t = q_ref.shape[1]
    wc = kbuf_ref.shape[1]
    kv_heads = range(SW_KV_HEADS)
    groups = []
    for i in range(bs):
        q = q_ref[i]
        qn = q * _seg_rms_scale(q, SW_HEADS)
        kn = k_ref[i]
        kn = kn * _seg_rms_scale(kn, SW_KV_HEADS) * kn_ref[...]
        v = v_ref[i]
        kbuf = kbuf_ref[i]
        vbuf = vbuf_ref[i]
        kbs = _head_cols(kbuf * kq_ref[...], kv_heads)
        kns = _head_cols(kn * kq_ref[...], kv_heads)
        vbs = _head_cols(vbuf, kv_heads)
        vns = _head_cols(v, kv_heads)
        for g in kv_heads:
            groups.append((
                _head_cols(qn, range(SW_GROUP * g, SW_GROUP * (g + 1))),
                [kbs[g].astype(BF16), kns[g].astype(BF16)],
                [vbs[g].astype(BF16), vns[g].astype(BF16)],
                [bb_ref[g], bn_ref[g]],
                sink_ref[g]))
        kout_ref[i, 0:wc - t, :] = kbuf[t:, :]
        kout_ref[i, wc - t:wc, :] = kn
        vout_ref[i, 0:wc - t, :] = vbuf[t:, :]
        vout_ref[i, wc - t:wc, :] = v
    outs = _attn_groups(groups)
    for i in range(bs):
        o_ref[i] = jnp.concatenate(outs[SW_HEADS * i:SW_HEADS * (i + 1)], axis=-1).astype(o_ref.dtype)


def _swa_sample(y3, layer, k_buf, v_buf, q_norm, k_norm, sink_col, bias_buf, bias_new, col_q, bs):
    batch, t, _ = y3.shape
    wc = k_buf.shape[2]
    cq = col_q // SW_Q
    ck = (col_q + SW_Q) // SW_KV
    cv = ck + 1
    return pl.pallas_call(
        functools.partial(_swa_sample_kernel, bs=bs),
        grid=(batch // bs,),
        in_specs=[
            pl.BlockSpec((bs, t, SW_Q), lambda b: (b, 0, cq)),
            pl.BlockSpec((bs, t, SW_KV), lambda b: (b, 0, ck)),
            pl.BlockSpec((bs, t, SW_KV), lambda b: (b, 0, cv)),
            pl.BlockSpec((None, bs, wc, SW_KV), lambda b: (layer, b, 0, 0)),
            pl.BlockSpec((None, bs, wc, SW_KV), lambda b: (layer, b, 0, 0)),
            pl.BlockSpec((None, 1, SW_KV), lambda b: (layer, 0, 0)),
            pl.BlockSpec((None, 1, SW_KV), lambda b: (layer, 0, 0)),
            pl.BlockSpec((None, SW_KV_HEADS, SW_GROUP * t, 1), lambda b: (layer, 0, 0, 0)),
            pl.BlockSpec((SW_KV_HEADS, SW_GROUP * t, wc), lambda b: (0, 0, 0)),
            pl.BlockSpec((SW_KV_HEADS, SW_GROUP * t, t), lambda b: (0, 0, 0)),
        ],
        out_specs=[
            pl.BlockSpec((bs, t, SW_Q), lambda b: (b, 0, 0)),
            pl.BlockSpec((bs, wc, SW_KV), lambda b: (b, 0, 0)),
            pl.BlockSpec((bs, wc, SW_KV), lambda b: (b, 0, 0)),
        ],
        out_shape=[
            jax.ShapeDtypeStruct((batch, t, SW_Q), F32),
            jax.ShapeDtypeStruct((batch, wc, SW_KV), F32),
            jax.ShapeDtypeStruct((batch, wc, SW_KV), F32),
        ],
        compiler_params=_params("arbitrary"),
        name="swa_sample",
    )(y3, y3, y3, k_buf, v_buf, q_norm, k_norm, sink_col, bias_buf, bias_new)


def _mixout_kernel(x_ref, oa_ref, ob_ref, ga_ref, gb_ref, g_ref, wdn_ref, wsw_ref, wo_ref, o_ref):
    ya = _dot(oa_ref[...].astype(BF16), wdn_ref[...])
    yb = _dot(ob_ref[...].astype(BF16), wsw_ref[...])
    merged = _sigmoid(ga_ref[...]) * ya + _sigmoid(gb_ref[...]) * yb
    o_ref[...] = x_ref[...] + g_ref[...] * _dot(merged.astype(BF16), wo_ref[...])


def _mixout(x, o_a, o_b, y, mod, layer, w_dn, w_sw, w_o, tm):
    m, d = x.shape
    tiles_per_row = (m // mod.shape[1]) // tm
    cga = COL_GA // d
    const = dict(pipeline_mode=pl.Buffered(1))
    return pl.pallas_call(
        _mixout_kernel,
        grid=(m // tm,),
        in_specs=[
            pl.BlockSpec((tm, d), lambda i: (i, 0)),
            pl.BlockSpec((tm, DN_V), lambda i: (i, 0)),
            pl.BlockSpec((tm, SW_Q), lambda i: (i, 0)),
            pl.BlockSpec((tm, d), lambda i: (i, cga)),
            pl.BlockSpec((tm, d), lambda i: (i, cga + 1)),
            pl.BlockSpec((None, None, mod.shape[2], d), lambda i: (layer, i // tiles_per_row, 0, 5)),
            pl.BlockSpec((None, DN_V, d), lambda i: (layer, 0, 0), **const),
            pl.BlockSpec((None, SW_Q, d), lambda i: (layer, 0, 0), **const),
            pl.BlockSpec((None, d, d), lambda i: (layer, 0, 0), **const),
        ],
        out_specs=pl.BlockSpec((tm, d), lambda i: (i, 0)),
        out_shape=jax.ShapeDtypeStruct((m, d), F32),
        compiler_params=_params("arbitrary", vmem_limit_bytes=LARGE_TILE_VMEM_LIMIT_BYTES),
        name="mixout",
    )(x, o_a, o_b, y, y, mod, w_dn, w_sw, w_o)


_BD_TAIL_BLOCKS = 3
TM_PROMPT = 1024
TF = 256
TF_SAMPLE = 512
TN = 1024
TM_MIX = 512
GDN_CHUNKS_PER_STEP = 4
GDN_SAMPLE_SEQS_PER_STEP = 4
SWA_Q_BLOCKS_PER_STEP = 2
SAMPLE_BATCH_PER_STEP = 4


def _reorder_kernel(w_ref, o_ref, *, segments):
    row = 0
    for start, height in segments:
        o_ref[row:row + height, :] = w_ref[start:start + height, :].astype(o_ref.dtype)
        row += height
    o_ref[row:, :] = jnp.zeros((o_ref.shape[0] - row, o_ref.shape[1]), o_ref.dtype)


def _reorder_w_in(w_in, d, tk=256):
    sizes = (DN_CONV_CH, DN_V, DN_HEADS, DN_HEADS, SW_Q, SW_KV, SW_KV, d, d)
    o = [int(v) for v in np.concatenate([[0], np.cumsum(sizes)])]
    segments = ((0, o[2]), (o[7], 2 * d), (o[4], o[7] - o[4]), (o[2], o[4] - o[2]))
    w_t = jnp.swapaxes(w_in, 1, 2)
    depth, n_in, rows = w_t.shape
    n_out = sum(h for _, h in segments) + BD_WIDTH - 2 * DN_HEADS + _BD_TAIL_BLOCKS * LANES
    return pl.pallas_call(
        functools.partial(_reorder_kernel, segments=segments),
        grid=(depth, rows // tk),
        in_specs=[pl.BlockSpec((None, n_in, tk), lambda l, i: (l, 0, i))],
        out_specs=pl.BlockSpec((None, n_out, tk), lambda l, i: (l, 0, i)),
        out_shape=jax.ShapeDtypeStruct((depth, n_out, rows), BF16),
        compiler_params=_params("arbitrary", "arbitrary"),
        name="reorder_w_in",
    )(w_t)


def kernel(x_prompt, x_sample, c_prompt, c_sample, state_delta, state_conv, cache_swa_k, cache_swa_v, rel_bias, w_ada, b_ada, norm_ffn1, w_ffn1_in, w_ffn1_out, norm_mix, w_in, dn_conv_w, dn_a_log, dn_dt_bias, dn_out_norm, w_dn_out, sw_q_norm, sw_k_norm, sw_sinks, w_sw_out, w_o, norm_ffn2, w_ffn2_in, w_ffn2_out):
    bp, seq, d = x_prompt.shape
    bd, dseq, _ = x_sample.shape
    depth = w_ada.shape[0]
    wc = cache_swa_k.shape[2]
    col_q = COL_GA + 2 * d

    n_rows = bp + bd
    pad_rows = -n_rows % SUBLANES
    c_all = jnp.concatenate([c_prompt, c_sample, jnp.zeros((pad_rows, d), F32)], axis=0)
    mod_all = _ada(c_all, w_ada, b_ada)
    mod_p = mod_all[:, :bp].reshape(depth, bp, 1, N_MOD * d)
    mod_s = jnp.repeat(mod_all[:, bp:n_rows], dseq, axis=1).reshape(depth, 1, bd * dseq, N_MOD * d)

    qi = np.arange(SW_BLOCK)[:, None]
    kj = np.arange(2 * SW_BLOCK)[None, :]
    bias_p = _bias_grid(rel_bias, _bucket_map(qi + SW_BLOCK - kj))
    ti = np.arange(dseq)[:, None]
    sj = np.arange(wc + dseq)[None, :]
    dist_s = ti + wc - sj
    bias_sb = _bias_grid(rel_bias, _bucket_map(dist_s[:, :wc]))
    bias_sn = _bias_grid(rel_bias, _bucket_map(dist_s[:, wc:]))

    w_in_r = _reorder_w_in(w_in, d)
    w_dn16 = w_dn_out.astype(BF16)
    w_sw16 = w_sw_out.astype(BF16)
    w_o16 = w_o.astype(BF16)
    r3 = lambda a: a.reshape(depth, 1, a.shape[-1])
    lane_row = lambda a: jnp.pad(a, ((0, 0), (DN_HEADS, LANES - 2 * DN_HEADS))).reshape(depth, 1, LANES)
    alog_row = lane_row(dn_a_log)
    dt_row = lane_row(dn_dt_bias)
    sinks = sw_sinks.reshape(depth, SW_KV_HEADS, SW_GROUP, 1)
    sink_p = jnp.repeat(sinks, SW_BLOCK, axis=2)
    sink_s = jnp.repeat(sinks, dseq, axis=2)
    kbuf = cache_swa_k.reshape(depth, bd, wc, SW_KV)
    vbuf = cache_swa_v.reshape(depth, bd, wc, SW_KV)
    g_ffn1, g_mix, g_ffn2, g_on = r3(norm_ffn1), r3(norm_mix), r3(norm_ffn2), r3(dn_out_norm)
    g_qn = r3(jnp.tile(sw_q_norm * SW_SCALE, (1, SW_KV_HEADS)))
    g_kn = r3(jnp.tile(sw_k_norm, (1, SW_KV_HEADS)))

    xp = x_prompt.reshape(bp * seq, d)
    xs = x_sample.reshape(bd * dseq, d)
    ms = bd * dseq
    outs = [[] for _ in range(8)]
    for l in range(depth):
        xp = _ffn(xp, mod_p, 0, l, g_ffn1, w_ffn1_in, w_ffn1_out, TM_PROMPT, TF)
        yp = _inproj(xp, mod_p, l, g_mix, w_in_r, TM_PROMPT, TN)
        oa, s_p, conv_p = _gdn(yp, bp, l, dn_conv_w, alog_row, dt_row, g_on, None, None, DN_CHUNK,
                               GDN_CHUNKS_PER_STEP, 1, BF16)
        ob, k_p, v_p = _swa_prompt(yp, bp, l, g_qn, g_kn, sink_p, bias_p, col_q, BF16, SWA_Q_BLOCKS_PER_STEP)
        xp = _mixout(xp, oa, ob, yp, mod_p, l, w_dn16, w_sw16, w_o16, TM_MIX)
        xp = _ffn(xp, mod_p, 6, l, g_ffn2, w_ffn2_in, w_ffn2_out, TM_PROMPT, TF)
        xs = _ffn(xs, mod_s, 0, l, g_ffn1, w_ffn1_in, w_ffn1_out, ms, TF_SAMPLE)
        ys = _inproj(xs, mod_s, l, g_mix, w_in_r, ms, TN)
        oa, s_s, conv_s = _gdn(ys, bd, l, dn_conv_w, alog_row, dt_row, g_on, state_conv, state_delta, dseq, 1,
                               GDN_SAMPLE_SEQS_PER_STEP, F32)
        ob, k_s, v_s = _swa_sample(ys.reshape(bd, dseq, -1), l, kbuf, vbuf, g_qn, g_kn, sink_s, bias_sb, bias_sn,
                                   col_q, SAMPLE_BATCH_PER_STEP)
        xs = _mixout(xs, oa, ob.reshape(ms, SW_Q), ys, mod_s, l, w_dn16, w_sw16, w_o16, ms)
        xs = _ffn(xs, mod_s, 6, l, g_ffn2, w_ffn2_in, w_ffn2_out, ms, TF_SAMPLE)
        for lst, val in zip(outs, (s_p, conv_p, k_p, v_p, s_s, conv_s, k_s, v_s)):
            lst.append(val)
    s_p, conv_p, k_p, v_p, s_s, conv_s, k_s, v_s = (jnp.stack(o) for o in outs)
    kv = lambda a: a.reshape(a.shape[:3] + (SW_KV_HEADS, SW_HD))
    return (xp.reshape(bp, seq, d), xs.reshape(bd, dseq, d), s_p, conv_p, kv(k_p), kv(v_p),
            s_s, conv_s, kv(k_s), kv(v_s))
```

```python
import functools
import math

import numpy as np
import jax
import jax.numpy as jnp
from jax import lax
from jax.experimental import pallas as pl
from jax.experimental.pallas import tpu as pltpu

F32 = jnp.float32
BF16 = jnp.bfloat16

NORM_EPS = 1e-6
NEG_INF = -1e30
N_MOD = 9

DN_HEADS = 8
DN_DK = 128
DN_DV = 128
DN_CONV = 4
DN_CHUNK = 64
DN_QK = DN_HEADS * DN_DK
DN_V = DN_HEADS * DN_DV
DN_CONV_CH = 2 * DN_QK + DN_V
DN_SCALE = DN_DK ** -0.5

SW_HEADS = 16
SW_KV_HEADS = 4
SW_HD = 64
SW_GROUP = SW_HEADS // SW_KV_HEADS
SW_SCALE = SW_HD ** -0.5
WINDOW = 128
SW_BLOCK = 128
SW_Q = SW_HEADS * SW_HD
SW_KV = SW_KV_HEADS * SW_HD

N_BUCKETS = 32
MAX_DISTANCE = 128

LANES = 128
SUBLANES = 8
VMEM_LIMIT_BYTES = 56 * 1024 * 1024
LARGE_TILE_VMEM_LIMIT_BYTES = 60 * 1024 * 1024

COL_CONV = 0
COL_Z = COL_CONV + DN_CONV_CH
COL_GA = COL_Z + DN_V
BD_WIDTH = LANES


def _sigmoid(x):
    return 1.0 / (1.0 + jnp.exp(-x))


def _silu(x):
    return x * _sigmoid(x)


def _softplus(x):
    return jnp.maximum(x, 0.0) + jnp.log1p(jnp.exp(-jnp.abs(x)))


def _dot(a, b, **kw):
    return jnp.dot(a, b, preferred_element_type=F32, **kw)


def _dot_nt(a, b):
    return lax.dot_general(a, b, (((1,), (1,)), ((), ())), preferred_element_type=F32)


def _dot_tn(a, b):
    return lax.dot_general(a, b, (((0,), (0,)), ((), ())), preferred_element_type=F32)


def _params(*semantics, vmem_limit_bytes=VMEM_LIMIT_BYTES):
    return pltpu.CompilerParams(dimension_semantics=semantics, vmem_limit_bytes=vmem_limit_bytes)


def _mod_norm(x, gain, sc, sh):
    r = lax.rsqrt(jnp.mean(x * x, axis=-1, keepdims=True) + NORM_EPS)
    if sc.shape[0] == 1:
        return (x * r) * (gain * (1.0 + sc)) + sh
    return x * r * gain * (1.0 + sc) + sh


def _ada_kernel(c_ref, w_ref, b_ref, o_ref):
    a = _silu(c_ref[...]).astype(BF16)
    o_ref[...] = _dot(a, w_ref[...].astype(BF16)) + b_ref[...]


def _ada(c_all, w_ada, b_ada, tn=1024):
    depth, d, n = w_ada.shape
    rows = c_all.shape[0]
    return pl.pallas_call(
        _ada_kernel,
        grid=(depth, n // tn),
        in_specs=[
            pl.BlockSpec((rows, d), lambda l, j: (0, 0)),
            pl.BlockSpec((None, d, tn), lambda l, j: (l, 0, j)),
            pl.BlockSpec((None, 1, tn), lambda l, j: (l, 0, j)),
        ],
        out_specs=pl.BlockSpec((None, rows, tn), lambda l, j: (l, 0, j)),
        out_shape=jax.ShapeDtypeStruct((depth, rows, n), F32),
        compiler_params=_params("arbitrary", "arbitrary"),
        name="ada",
    )(c_all, w_ada, b_ada.reshape(depth, 1, n))


def _bucket_table():
    max_exact = N_BUCKETS // 2
    d = np.arange(WINDOW + 1)
    ratio = np.log(np.maximum(d, 1).astype(np.float32) / np.float32(max_exact)) / np.float32(math.log(MAX_DISTANCE / max_exact))
    large = max_exact + (ratio.astype(np.float32) * np.float32(N_BUCKETS - max_exact)).astype(np.int32)
    return np.where(d < max_exact, d, np.minimum(large, N_BUCKETS - 1)).astype(np.int32)


def _bucket_map(dist):
    table = _bucket_table()
    valid = (dist >= 0) & (dist <= WINDOW)
    return np.where(valid, table[np.clip(dist, 0, WINDOW)], -1).astype(np.int32)


def _bias_kernel(rb_ref, bm_ref, o_ref):
    bm = bm_ref[...]
    tq = bm.shape[0]
    for g in range(SW_KV_HEADS):
        for j in range(SW_GROUP):
            acc = jnp.full(bm.shape, NEG_INF, F32)
            for b in range(N_BUCKETS):
                acc = jnp.where(bm == b, rb_ref[b, g * SW_GROUP + j], acc)
            o_ref[g, j * tq:(j + 1) * tq, :] = acc


def _bias_grid(rel_bias, bmap):
    tq, tk = bmap.shape
    return pl.pallas_call(
        _bias_kernel,
        in_specs=[
            pl.BlockSpec(memory_space=pltpu.SMEM),
            pl.BlockSpec((tq, tk), lambda: (0, 0)),
        ],
        out_specs=pl.BlockSpec((SW_KV_HEADS, SW_GROUP * tq, tk), lambda: (0, 0, 0)),
        out_shape=jax.ShapeDtypeStruct((SW_KV_HEADS, SW_GROUP * tq, tk), F32),
        name="relbias",
    )(rel_bias, jnp.asarray(bmap))


def _ffn_kernel(x_ref, sh_ref, sc_ref, g_ref, gain_ref, wg_ref, wu_ref, wo_ref, o_ref, h_ref, *, nf):
    f = pl.program_id(1)

    @pl.when(f == 0)
    def _():
        h_ref[...] = _mod_norm(x_ref[...], gain_ref[...], sc_ref[...], sh_ref[...]).astype(BF16)
        o_ref[...] = jnp.zeros_like(o_ref)

    h = h_ref[...]
    gate = _dot(h, wg_ref[...].astype(BF16))
    up = _dot(h, wu_ref[...].astype(BF16))
    a = (_silu(gate) * up).astype(BF16)
    o_ref[...] += _dot(a, wo_ref[...].astype(BF16))

    @pl.when(f == nf - 1)
    def _():
        o_ref[...] = x_ref[...] + (0.5 * g_ref[...]) * o_ref[...]


def _mod_spec(mod, layer, seg, d, tiles_per_row):
    r = mod.shape[2]
    return pl.BlockSpec((None, None, r, d), lambda i, j: (layer, i // tiles_per_row, 0, seg))


def _ffn(x, mod, seg0, layer, gain, w_in, w_out, tm, tf):
    m, d = x.shape
    ff = w_out.shape[1]
    nf = ff // tf
    tiles_per_row = (m // mod.shape[1]) // tm
    return pl.pallas_call(
        functools.partial(_ffn_kernel, nf=nf),
        grid=(m // tm, nf),
        in_specs=[
            pl.BlockSpec((tm, d), lambda i, f: (i, 0)),
            _mod_spec(mod, layer, seg0, d, tiles_per_row),
            _mod_spec(mod, layer, seg0 + 1, d, tiles_per_row),
            _mod_spec(mod, layer, seg0 + 2, d, tiles_per_row),
            pl.BlockSpec((None, 1, d), lambda i, f: (layer, 0, 0)),
            pl.BlockSpec((None, d, tf), lambda i, f: (layer, 0, f)),
            pl.BlockSpec((None, d, tf), lambda i, f: (layer, 0, f + nf)),
            pl.BlockSpec((None, tf, d), lambda i, f: (layer, f, 0)),
        ],
        out_specs=pl.BlockSpec((tm, d), lambda i, f: (i, 0)),
        out_shape=jax.ShapeDtypeStruct((m, d), F32),
        scratch_shapes=[pltpu.VMEM((tm, d), BF16)],
        compiler_params=_params("arbitrary", "arbitrary", vmem_limit_bytes=LARGE_TILE_VMEM_LIMIT_BYTES),
        name="ffn",
    )(x, mod, mod, mod, gain, w_in, w_in, w_out)


def _inproj_kernel(x_ref, sh_ref, sc_ref, gain_ref, w_ref, o_ref, h_ref):
    @pl.when(pl.program_id(1) == 0)
    def _():
        h_ref[...] = _mod_norm(x_ref[...], gain_ref[...], sc_ref[...], sh_ref[...]).astype(BF16)

    o_ref[...] = _dot_nt(h_ref[...], w_ref[...])


def _inproj(x, mod, layer, gain, w_r, tm, tn):
    m, d = x.shape
    nr = w_r.shape[1]
    tiles_per_row = (m // mod.shape[1]) // tm
    return pl.pallas_call(
        _inproj_kernel,
        grid=(m // tm, nr // tn),
        in_specs=[
            pl.BlockSpec((tm, d), lambda i, j: (i, 0)),
            _mod_spec(mod, layer, 3, d, tiles_per_row),
            _mod_spec(mod, layer, 4, d, tiles_per_row),
            pl.BlockSpec((None, 1, d), lambda i, j: (layer, 0, 0)),
            pl.BlockSpec((None, tn, d), lambda i, j: (layer, j, 0)),
        ],
        out_specs=pl.BlockSpec((tm, tn), lambda i, j: (i, j)),
        out_shape=jax.ShapeDtypeStruct((m, nr), F32),
        scratch_shapes=[pltpu.VMEM((tm, d), BF16)],
        compiler_params=_params("arbitrary", "arbitrary"),
        name="inproj",
    )(x, mod, mod, gain, w_r)


def _unit_lower_inverses(lmats, ri, ci):
    c = lmats[0].shape[0]
    n = len(lmats)
    eye = (ri == ci).astype(F32)
    if c == SUBLANES:
        ts = [eye for _ in lmats]
        for j in range(c - 1):
            ts = [t - l[:, j:j + 1] * t[j:j + 1, :] for t, l in zip(ts, lmats)]
        return ts
    pair = (jnp.right_shift(ri, 1) == jnp.right_shift(ci, 1)) & (ri == ci + 1)
    ts = [eye - jnp.where(pair, l, 0.0) for l in lmats]
    shift = 1
    while (1 << shift) < c:
        rb = jnp.right_shift(ri, shift)
        cb = jnp.right_shift(ci, shift)
        off = (jnp.right_shift(rb, 1) == jnp.right_shift(cb, 1)) & (rb == cb + 1)
        t16 = [t.astype(BF16) for t in ts]
        ps = [_dot(t16[h], jnp.where(off, lmats[h], 0.0).astype(BF16)).astype(BF16) for h in range(n)]
        xs = [_dot(ps[h], t16[h]) for h in range(n)]
        ts = [ts[h] - xs[h] for h in range(n)]
        shift += 1
    return ts


def _gdn_kernel(*refs, chunk, chunks_per_step, seqs, has_init):
    if has_init:
        (xc_ref, z_ref, bd_ref, cw_ref, alog_ref, dt_ref, on_ref, conv0_ref, s0_ref,
         o_ref, s_ref, conv_ref, xs_ref) = refs
    else:
        (xc_ref, z_ref, bd_ref, cw_ref, alog_ref, dt_ref, on_ref,
         o_ref, s_ref, conv_ref, xs_ref) = refs
    t = pl.program_id(1)
    nt = pl.num_programs(1)
    c = chunk
    tb = c * chunks_per_step
    hist = SUBLANES
    stride = hist + tb
    seq_ids = range(seqs)

    @pl.when(t == 0)
    def _():
        for s in seq_ids:
            xs_ref[stride * s:stride * s + hist, :] = jnp.zeros((hist, DN_CONV_CH), F32)
            if has_init:
                xs_ref[stride * s + hist - (DN_CONV - 1):stride * s + hist, :] = conv0_ref[s]
        if has_init:
            s_ref[...] = s0_ref[...]
        else:
            s_ref[...] = jnp.zeros_like(s_ref)

    accs = []
    for s in seq_ids:
        base = stride * s + hist
        xs_ref[base:base + tb, :] = xc_ref[tb * s:tb * (s + 1), :]
        acc = xs_ref[base:base + tb, :] * cw_ref[DN_CONV - 1:DN_CONV, :]
        for i in range(DN_CONV - 1):
            off = base - (DN_CONV - 1) + i
            acc = acc + xs_ref[off:off + tb, :] * cw_ref[i:i + 1, :]
        accs.append(acc)
    xc_all = _silu(accs[0] if seqs == 1 else jnp.concatenate(accs, axis=0))

    @pl.when(t == nt - 1)
    def _():
        for s in seq_ids:
            conv_ref[s] = xs_ref[stride * (s + 1) - (DN_CONV - 1):stride * (s + 1), :]

    for s in seq_ids:
        xs_ref[stride * s:stride * s + hist, :] = xs_ref[stride * s + tb:stride * s + tb + hist, :]

    bd = bd_ref[...]
    beta_blk = _sigmoid(bd)
    g_blk = -jnp.exp(alog_ref[...]) * _softplus(bd + dt_ref[...])
    ri = lax.broadcasted_iota(jnp.int32, (c, c), 0)
    ci = lax.broadcasted_iota(jnp.int32, (c, c), 1)
    incl = ri >= ci
    strict = ri > ci
    tri = incl.astype(F32)

    on = on_ref[...]
    heads = range(DN_HEADS)
    blocks = [(s, j) for s in seq_ids for j in range(chunks_per_step)]
    q_all = xc_all[:, 0:DN_QK]
    k_all = xc_all[:, DN_QK:2 * DN_QK]
    rows_all = seqs * tb
    if tb >= DN_CHUNK:
        q_scale = _seg_rms_scale(q_all, DN_HEADS, mean=False)
        k_scale = _seg_rms_scale(k_all, DN_HEADS, mean=False)
    else:
        q_scale, k_scale = (jnp.concatenate(
            [jnp.broadcast_to(lax.rsqrt(jnp.sum(jnp.square(a[:, DN_DK * h:DN_DK * (h + 1)]), axis=-1, keepdims=True)
                                        + NORM_EPS), (rows_all, DN_DK)) for h in heads], axis=-1) for a in (q_all, k_all))
    qn_all = q_all * (q_scale * DN_SCALE)
    kn_all = k_all * k_scale
    qs, ks, vs, betas, gcols, glasts, decays, egcs = [], [], [], [], [], [], [], []
    for s, j in blocks:
        rows = slice(tb * s + c * j, tb * s + c * (j + 1))
        xc = xc_all[rows, :]
        qn = qn_all[rows, :]
        kn = kn_all[rows, :]
        beta_all = beta_blk[rows, :]
        gc_all = _dot(tri, g_blk[rows, :], precision=lax.Precision.HIGHEST)
        gc_pad = gc_all if c == LANES else jnp.concatenate([gc_all, jnp.zeros((LANES - c, LANES), F32)], axis=0)
        gc_t = gc_pad.T
        for h in heads:
            qs.append(qn[:, DN_DK * h:DN_DK * (h + 1)])
            ks.append(kn[:, DN_DK * h:DN_DK * (h + 1)])
            vs.append(xc[:, 2 * DN_QK + DN_DV * h:2 * DN_QK + DN_DV * (h + 1)])
            betas.append(beta_all[:, h:h + 1])
            gcol = gc_all[:, DN_HEADS + h:DN_HEADS + h + 1]
            grow = gc_t[DN_HEADS + h:DN_HEADS + h + 1, 0:c]
            gcols.append(gcol)
            glasts.append(gc_all[c - 1:c, DN_HEADS + h:DN_HEADS + h + 1])
            decays.append(jnp.exp(jnp.where(incl, gcol - grow, NEG_INF)))
            egcs.append(jnp.exp(gcol))
    n = len(blocks) * DN_HEADS
    kbs = [ks[i] * betas[i] for i in range(n)]
    k16 = [ks[i].astype(BF16) for i in range(n)]
    kk = [_dot_nt(kbs[i].astype(BF16), k16[i]) for i in range(n)]
    qk = [_dot_nt(qs[i].astype(BF16), k16[i]) for i in range(n)]
    lmats = [jnp.where(strict, kk[i] * decays[i], 0.0) for i in range(n)]
    tinv = _unit_lower_inverses(lmats, ri, ci)
    rhs = [jnp.concatenate([vs[i] * betas[i], kbs[i] * egcs[i]], axis=-1).astype(BF16) for i in range(n)]
    sol = [_dot(tinv[i].astype(BF16), rhs[i]) for i in range(n)]
    wq = [jnp.concatenate([sol[i][:, DN_DV:], qs[i] * egcs[i]], axis=0).astype(BF16) for i in range(n)]
    a16 = [(qk[i] * decays[i]).astype(BF16) for i in range(n)]
    kdec = [(ks[i] * jnp.exp(glasts[i] - gcols[i])).astype(BF16) for i in range(n)]
    lanes = [(s, h) for s in seq_ids for h in heads]
    state = [s_ref[s, h] for s, h in lanes]
    for j in range(chunks_per_step):
        idx = [(s * chunks_per_step + j) * DN_HEADS + h for s, h in lanes]
        m = range(len(lanes))
        s16 = [st.astype(BF16) for st in state]
        wqs = [_dot(wq[idx[a]], s16[a]) for a in m]
        v16 = [(sol[idx[a]][:, :DN_DV] - wqs[a][:c, :]).astype(BF16) for a in m]
        av = [_dot(a16[idx[a]], v16[a]) for a in m]
        kv = [_dot_tn(kdec[idx[a]], v16[a]) for a in m]
        state = [state[a] * jnp.exp(glasts[idx[a]]) + kv[a] for a in m]
        for a, (s, h) in enumerate(lanes):
            o = wqs[a][c:, :] + av[a]
            o = o * lax.rsqrt(jnp.mean(o * o, axis=-1, keepdims=True) + NORM_EPS) * on
            rows = slice(tb * s + c * j, tb * s + c * (j + 1))
            zh = z_ref[rows, DN_DV * h:DN_DV * (h + 1)]
            o_ref[rows, DN_DV * h:DN_DV * (h + 1)] = (o * _silu(zh)).astype(o_ref.dtype)
    for a, (s, h) in enumerate(lanes):
        s_ref[s, h] = state[a]


def _gdn(y, batch, layer, conv_w, alog_row, dt_row, out_norm, conv0, s0, chunk, chunks_per_step, seqs, out_dtype):
    m = y.shape[0]
    seq = m // batch
    tb = chunk * chunks_per_step
    nt = seq // tb
    assert seqs == 1 or nt == 1
    has_init = conv0 is not None
    row = lambda b, t: b * nt + t
    rb = seqs * tb
    in_specs = [
        pl.BlockSpec((rb, DN_CONV_CH), lambda b, t: (row(b, t), COL_CONV // DN_CONV_CH)),
        pl.BlockSpec((rb, DN_V), lambda b, t: (row(b, t), COL_Z // DN_V)),
        pl.BlockSpec((rb, BD_WIDTH), lambda b, t: (row(b, t), y.shape[1] // BD_WIDTH - 1 - _BD_TAIL_BLOCKS)),
        pl.BlockSpec((None, DN_CONV, DN_CONV_CH), lambda b, t: (layer, 0, 0)),
        pl.BlockSpec((None, 1, LANES), lambda b, t: (layer, 0, 0)),
        pl.BlockSpec((None, 1, LANES), lambda b, t: (layer, 0, 0)),
        pl.BlockSpec((None, 1, DN_DV), lambda b, t: (layer, 0, 0)),
    ]
    args = [y, y, y, conv_w, alog_row, dt_row, out_norm]
    if has_init:
        in_specs += [
            pl.BlockSpec((None, seqs, DN_CONV - 1, DN_CONV_CH), lambda b, t: (layer, b, 0, 0)),
            pl.BlockSpec((None, seqs, DN_HEADS, DN_DK, DN_DV), lambda b, t: (layer, b, 0, 0, 0)),
        ]
        args += [conv0, s0]
    return pl.pallas_call(
        functools.partial(_gdn_kernel, chunk=chunk, chunks_per_step=chunks_per_step, seqs=seqs, has_init=has_init),
        grid=(batch // seqs, nt),
        in_specs=in_specs,
        out_specs=[
            pl.BlockSpec((rb, DN_V), lambda b, t: (row(b, t), 0)),
            pl.BlockSpec((seqs, DN_HEADS, DN_DK, DN_DV), lambda b, t: (b, 0, 0, 0)),
            pl.BlockSpec((seqs, DN_CONV - 1, DN_CONV_CH), lambda b, t: (b, 0, 0)),
        ],
        out_shape=[
            jax.ShapeDtypeStruct((m, DN_V), out_dtype),
            jax.ShapeDtypeStruct((batch, DN_HEADS, DN_DK, DN_DV), F32),
            jax.ShapeDtypeStruct((batch, DN_CONV - 1, DN_CONV_CH), F32),
        ],
        scratch_shapes=[pltpu.VMEM((seqs * (tb + SUBLANES), DN_CONV_CH), F32)],
        compiler_params=_params("arbitrary", "arbitrary"),
        name="gdn",
    )(*args)


def _seg_rms_scale(x, heads, mean=True):
    w = x.shape[1]
    hd = w // heads
    shift = int(math.log2(hd))
    norm = 1.0 / hd if mean else 1.0
    seg = (jnp.right_shift(lax.broadcasted_iota(jnp.int32, (w, LANES), 0), shift)
           == lax.broadcasted_iota(jnp.int32, (w, LANES), 1)).astype(BF16)
    seg_t = (lax.broadcasted_iota(jnp.int32, (LANES, w), 0)
             == jnp.right_shift(lax.broadcasted_iota(jnp.int32, (LANES, w), 1), shift)).astype(BF16)
    scale = lax.rsqrt(_dot((x * x).astype(BF16), seg) * norm + NORM_EPS)
    hi = scale.astype(BF16)
    lo = (scale - hi.astype(F32)).astype(BF16)
    return _dot(hi, seg_t) + _dot(lo, seg_t)


def _same_shape_fold(xs, op):
    acc = {}
    for x in xs:
        acc[x.shape] = x if x.shape not in acc else op(acc[x.shape], x)
    return list(acc.values())


def _attn_groups(groups):
    tq = groups[0][0][0].shape[0]
    qg = [jnp.concatenate(g[0], axis=0).astype(BF16) for g in groups]
    raw = [[_dot_nt(q, k) for k in g[1]] for q, g in zip(qg, groups)]
    probs, sink_terms = [], []
    for r, g in zip(raw, groups):
        sink = g[4]
        ss = [s + b for s, b in zip(r, g[3])]
        m = sink
        for s in _same_shape_fold(ss, jnp.maximum):
            m = jnp.maximum(m, jnp.max(s, axis=-1, keepdims=True))
        probs.append([jnp.exp(s - m).astype(BF16) for s in ss])
        sink_terms.append(jnp.exp(sink - m))
    pv = [[_dot(p, v) for p, v in zip(ps, g[2])] for ps, g in zip(probs, groups)]
    rowsum = [[_dot(p, jnp.ones((p.shape[1], LANES), BF16)) for p in ps] for ps in probs]
    outs = []
    for parts, sums, es in zip(pv, rowsum, sink_terms):
        o = parts[0]
        for part in parts[1:]:
            o = o + part
        den = sums[0]
        for part in sums[1:]:
            den = den + part
        o = o * (1.0 / (den + es))[:, :o.shape[1]]
        outs += [o[tq * j:tq * (j + 1), :] for j in range(SW_GROUP)]
    return outs


def _head_cols(x, heads):
    return [x[:, SW_HD * h:SW_HD * (h + 1)] for h in heads]


def _swa_prompt_kernel(q_ref, kp_ref, kc_ref, vp_ref, vc_ref, kq_ref, kn_ref, sink_ref, bias_ref,
                       o_ref, kout_ref, vout_ref, *, qb):
    n = pl.program_id(1)
    nb = pl.num_programs(1)
    kv_heads = range(SW_KV_HEADS)
    q = q_ref[...]
    qn = q * _seg_rms_scale(q, SW_HEADS)
    kc = kc_ref[...]
    kc = kc * _seg_rms_scale(kc, SW_KV_HEADS) * kn_ref[...]
    kp = kp_ref[...]
    kp = kp * _seg_rms_scale(kp, SW_KV_HEADS) * kn_ref[...]
    k_all = jnp.concatenate([kp, kc], axis=0) * kq_ref[...]
    vc = vc_ref[...]
    v_all = jnp.concatenate([vp_ref[...], vc], axis=0)
    first = n == 0
    groups = []
    for s in range(qb):
        blk = lambda a, i: a[SW_BLOCK * i:SW_BLOCK * (i + 1), :]
        qs = blk(qn, s)
        ks = [_head_cols(blk(k_all, s + i), kv_heads) for i in range(2)]
        vs = [_head_cols(blk(v_all, s + i), kv_heads) for i in range(2)]
        for g in kv_heads:
            bias = bias_ref[g]
            bias_prev = bias[:, :SW_BLOCK]
            if s == 0:
                bias_prev = jnp.where(first, NEG_INF, bias_prev)
            groups.append((
                _head_cols(qs, range(SW_GROUP * g, SW_GROUP * (g + 1))),
                [ks[0][g].astype(BF16), ks[1][g].astype(BF16)],
                [vs[0][g].astype(BF16), vs[1][g].astype(BF16)],
                [bias_prev, bias[:, SW_BLOCK:]],
                sink_ref[g]))
    outs = _attn_groups(groups)
    for s in range(qb):
        o_ref[SW_BLOCK * s:SW_BLOCK * (s + 1), :] = jnp.concatenate(
            outs[SW_HEADS * s:SW_HEADS * (s + 1)], axis=-1).astype(o_ref.dtype)

    @pl.when(n == nb - 1)
    def _():
        kout_ref[...] = kc[SW_BLOCK * (qb - 1):, :]
        vout_ref[...] = vc[SW_BLOCK * (qb - 1):, :]


def _swa_prompt(y, batch, layer, q_norm, k_norm, sink_col, bias, col_q, out_dtype, qb):
    m = y.shape[0]
    seq = m // batch
    tq = SW_BLOCK * qb
    nb = seq // tq
    cq = col_q // SW_Q
    ck = (col_q + SW_Q) // SW_KV
    cv = ck + 1
    cur = lambda b, n: b * nb + n
    prev = lambda b, n: (b * nb + n) * qb - jnp.minimum(n, 1)
    return pl.pallas_call(
        functools.partial(_swa_prompt_kernel, qb=qb),
        grid=(batch, nb),
        in_specs=[
            pl.BlockSpec((tq, SW_Q), lambda b, n: (cur(b, n), cq)),
            pl.BlockSpec((SW_BLOCK, SW_KV), lambda b, n: (prev(b, n), ck)),
            pl.BlockSpec((tq, SW_KV), lambda b, n: (cur(b, n), ck)),
            pl.BlockSpec((SW_BLOCK, SW_KV), lambda b, n: (prev(b, n), cv)),
            pl.BlockSpec((tq, SW_KV), lambda b, n: (cur(b, n), cv)),
            pl.BlockSpec((None, 1, SW_KV), lambda b, n: (layer, 0, 0)),
            pl.BlockSpec((None, 1, SW_KV), lambda b, n: (layer, 0, 0)),
            pl.BlockSpec((None, SW_KV_HEADS, SW_GROUP * SW_BLOCK, 1), lambda b, n: (layer, 0, 0, 0)),
            pl.BlockSpec((SW_KV_HEADS, SW_GROUP * SW_BLOCK, 2 * SW_BLOCK), lambda b, n: (0, 0, 0)),
        ],
        out_specs=[
            pl.BlockSpec((tq, SW_Q), lambda b, n: (cur(b, n), 0)),
            pl.BlockSpec((None, WINDOW, SW_KV), lambda b, n: (b, 0, 0)),
            pl.BlockSpec((None, WINDOW, SW_KV), lambda b, n: (b, 0, 0)),
        ],
        out_shape=[
            jax.ShapeDtypeStruct((m, SW_Q), out_dtype),
            jax.ShapeDtypeStruct((batch, WINDOW, SW_KV), F32),
            jax.ShapeDtypeStruct((batch, WINDOW, SW_KV), F32),
        ],
        compiler_params=_params("arbitrary", "arbitrary"),
        name="swa_prompt",
    )(y, y, y, y, y, q_norm, k_norm, sink_col, bias)


def _swa_sample_kernel(q_ref, k_ref, v_ref, kbuf_ref, vbuf_ref, kq_ref, kn_ref, sink_ref, bb_ref, bn_ref,
                       o_ref, kout_ref, vout_ref, *, bs):
    t = q_ref.shape[1]
    wc = kbuf_ref.shape[1]
    kv_heads = range(SW_KV_HEADS)
    groups = []
    for i in range(bs):
        q = q_ref[i]
        qn = q * _seg_rms_scale(q, SW_HEADS)
        kn = k_ref[i]
        kn = kn * _seg_rms_scale(kn, SW_KV_HEADS) * kn_ref[...]
        v = v_ref[i]
        kbuf = kbuf_ref[i]
        vbuf = vbuf_ref[i]
        kbs = _head_cols(kbuf * kq_ref[...], kv_heads)
        kns = _head_cols(kn * kq_ref[...], kv_heads)
        vbs = _head_cols(vbuf, kv_heads)
        vns = _head_cols(v, kv_heads)
        for g in kv_heads:
            groups.append((
                _head_cols(qn, range(SW_GROUP * g, SW_GROUP * (g + 1))),
                [kbs[g].astype(BF16), kns[g].astype(BF16)],
                [vbs[g].astype(BF16), vns[g].astype(BF16)],
                [bb_ref[g], bn_ref[g]],
                sink_ref[g]))
        kout_ref[i, 0:wc - t, :] = kbuf[t:, :]
        kout_ref[i, wc - t:wc, :] = kn
        vout_ref[i, 0:wc - t, :] = vbuf[t:, :]
        vout_ref[i, wc - t:wc, :] = v
    outs = _attn_groups(groups)
    for i in range(bs):
        o_ref[i] = jnp.concatenate(outs[SW_HEADS * i:SW_HEADS * (i + 1)], axis=-1).astype(o_ref.dtype)


def _swa_sample(y3, layer, k_buf, v_buf, q_norm, k_norm, sink_col, bias_buf, bias_new, col_q, bs):
    batch, t, _ = y3.shape
    wc = k_buf.shape[2]
    cq = col_q // SW_Q
    ck = (col_q + SW_Q) // SW_KV
    cv = ck + 1
    return pl.pallas_call(
        functools.partial(_swa_sample_kernel, bs=bs),
        grid=(batch // bs,),
        in_specs=[
            pl.BlockSpec((bs, t, SW_Q), lambda b: (b, 0, cq)),
            pl.BlockSpec((bs, t, SW_KV), lambda b: (b, 0, ck)),
            pl.BlockSpec((bs, t, SW_KV), lambda b: (b, 0, cv)),
            pl.BlockSpec((None, bs, wc, SW_KV), lambda b: (layer, b, 0, 0)),
            pl.BlockSpec((None, bs, wc, SW_KV), lambda b: (layer, b, 0, 0)),
            pl.BlockSpec((None, 1, SW_KV), lambda b: (layer, 0, 0)),
            pl.BlockSpec((None, 1, SW_KV), lambda b: (layer, 0, 0)),
            pl.BlockSpec((None, SW_KV_HEADS, SW_GROUP * t, 1), lambda b: (layer, 0, 0, 0)),
            pl.BlockSpec((SW_KV_HEADS, SW_GROUP * t, wc), lambda b: (0, 0, 0)),
            pl.BlockSpec((SW_KV_HEADS, SW_GROUP * t, t), lambda b: (0, 0, 0)),
        ],
        out_specs=[
            pl.BlockSpec((bs, t, SW_Q), lambda b: (b, 0, 0)),
            pl.BlockSpec((bs, wc, SW_KV), lambda b: (b, 0, 0)),
            pl.BlockSpec((bs, wc, SW_KV), lambda b: (b, 0, 0)),
        ],
        out_shape=[
            jax.ShapeDtypeStruct((batch, t, SW_Q), F32),
            jax.ShapeDtypeStruct((batch, wc, SW_KV), F32),
            jax.ShapeDtypeStruct((batch, wc, SW_KV), F32),
        ],
        compiler_params=_params("arbitrary"),
        name="swa_sample",
    )(y3, y3, y3, k_buf, v_buf, q_norm, k_norm, sink_col, bias_buf, bias_new)


def _mixout_kernel(x_ref, oa_ref, ob_ref, ga_ref, gb_ref, g_ref, wdn_ref, wsw_ref, wo_ref, o_ref):
    ya = _dot(oa_ref[...].astype(BF16), wdn_ref[...])
    yb = _dot(ob_ref[...].astype(BF16), wsw_ref[...])
    merged = _sigmoid(ga_ref[...]) * ya + _sigmoid(gb_ref[...]) * yb
    o_ref[...] = x_ref[...] + g_ref[...] * _dot(merged.astype(BF16), wo_ref[...])


def _mixout(x, o_a, o_b, y, mod, layer, w_dn, w_sw, w_o, tm):
    m, d = x.shape
    tiles_per_row = (m // mod.shape[1]) // tm
    cga = COL_GA // d
    const = dict(pipeline_mode=pl.Buffered(1))
    return pl.pallas_call(
        _mixout_kernel,
        grid=(m // tm,),
        in_specs=[
            pl.BlockSpec((tm, d), lambda i: (i, 0)),
            pl.BlockSpec((tm, DN_V), lambda i: (i, 0)),
            pl.BlockSpec((tm, SW_Q), lambda i: (i, 0)),
            pl.BlockSpec((tm, d), lambda i: (i, cga)),
            pl.BlockSpec((tm, d), lambda i: (i, cga + 1)),
            pl.BlockSpec((None, None, mod.shape[2], d), lambda i: (layer, i // tiles_per_row, 0, 5)),
            pl.BlockSpec((None, DN_V, d), lambda i: (layer, 0, 0), **const),
            pl.BlockSpec((None, SW_Q, d), lambda i: (layer, 0, 0), **const),
            pl.BlockSpec((None, d, d), lambda i: (layer, 0, 0), **const),
        ],
        out_specs=pl.BlockSpec((tm, d), lambda i: (i, 0)),
        out_shape=jax.ShapeDtypeStruct((m, d), F32),
        compiler_params=_params("arbitrary", vmem_limit_bytes=LARGE_TILE_VMEM_LIMIT_BYTES),
        name="mixout",
    )(x, o_a, o_b, y, y, mod, w_dn, w_sw, w_o)


_BD_TAIL_BLOCKS = 3
TM_PROMPT = 1024
TF = 256
TF_SAMPLE = 512
TN = 1024
TM_MIX = 512
GDN_CHUNKS_PER_STEP = 4
GDN_SAMPLE_SEQS_PER_STEP = 4
SWA_Q_BLOCKS_PER_STEP = 2
SAMPLE_BATCH_PER_STEP = 8


def _reorder_kernel(w_ref, o_ref, *, segments):
    row = 0
    for start, height in segments:
        o_ref[row:row + height, :] = w_ref[start:start + height, :].astype(o_ref.dtype)
        row += height
    o_ref[row:, :] = jnp.zeros((o_ref.shape[0] - row, o_ref.shape[1]), o_ref.dtype)


def _reorder_w_in(w_in, d, tk=256):
    sizes = (DN_CONV_CH, DN_V, DN_HEADS, DN_HEADS, SW_Q, SW_KV, SW_KV, d, d)
    o = [int(v) for v in np.concatenate([[0], np.cumsum(sizes)])]
    segments = ((0, o[2]), (o[7], 2 * d), (o[4], o[7] - o[4]), (o[2], o[4] - o[2]))
    w_t = jnp.swapaxes(w_in, 1, 2)
    depth, n_in, rows = w_t.shape
    n_out = sum(h for _, h in segments) + BD_WIDTH - 2 * DN_HEADS + _BD_TAIL_BLOCKS * LANES
    return pl.pallas_call(
        functools.partial(_reorder_kernel, segments=segments),
        grid=(depth, rows // tk),
        in_specs=[pl.BlockSpec((None, n_in, tk), lambda l, i: (l, 0, i))],
        out_specs=pl.BlockSpec((None, n_out, tk), lambda l, i: (l, 0, i)),
        out_shape=jax.ShapeDtypeStruct((depth, n_out, rows), BF16),
        compiler_params=_params("arbitrary", "arbitrary"),
        name="reorder_w_in",
    )(w_t)


def kernel(x_prompt, x_sample, c_prompt, c_sample, state_delta, state_conv, cache_swa_k, cache_swa_v, rel_bias, w_ada, b_ada, norm_ffn1, w_ffn1_in, w_ffn1_out, norm_mix, w_in, dn_conv_w, dn_a_log, dn_dt_bias, dn_out_norm, w_dn_out, sw_q_norm, sw_k_norm, sw_sinks, w_sw_out, w_o, norm_ffn2, w_ffn2_in, w_ffn2_out):
    bp, seq, d = x_prompt.shape
    bd, dseq, _ = x_sample.shape
    depth = w_ada.shape[0]
    wc = cache_swa_k.shape[2]
    col_q = COL_GA + 2 * d

    n_rows = bp + bd
    pad_rows = -n_rows % SUBLANES
    c_all = jnp.concatenate([c_prompt, c_sample, jnp.zeros((pad_rows, d), F32)], axis=0)
    mod_all = _ada(c_all, w_ada, b_ada)
    mod_p = mod_all[:, :bp].reshape(depth, bp, 1, N_MOD * d)
    mod_s = jnp.repeat(mod_all[:, bp:n_rows], dseq, axis=1).reshape(depth, 1, bd * dseq, N_MOD * d)

    qi = np.arange(SW_BLOCK)[:, None]
    kj = np.arange(2 * SW_BLOCK)[None, :]
    bias_p = _bias_grid(rel_bias, _bucket_map(qi + SW_BLOCK - kj))
    ti = np.arange(dseq)[:, None]
    sj = np.arange(wc + dseq)[None, :]
    dist_s = ti + wc - sj
    bias_sb = _bias_grid(rel_bias, _bucket_map(dist_s[:, :wc]))
    bias_sn = _bias_grid(rel_bias, _bucket_map(dist_s[:, wc:]))

    w_in_r = _reorder_w_in(w_in, d)
    w_dn16 = w_dn_out.astype(BF16)
    w_sw16 = w_sw_out.astype(BF16)
    w_o16 = w_o.astype(BF16)
    r3 = lambda a: a.reshape(depth, 1, a.shape[-1])
    lane_row = lambda a: jnp.pad(a, ((0, 0), (DN_HEADS, LANES - 2 * DN_HEADS))).reshape(depth, 1, LANES)
    alog_row = lane_row(dn_a_log)
    dt_row = lane_row(dn_dt_bias)
    sinks = sw_sinks.reshape(depth, SW_KV_HEADS, SW_GROUP, 1)
    sink_p = jnp.repeat(sinks, SW_BLOCK, axis=2)
    sink_s = jnp.repeat(sinks, dseq, axis=2)
    kbuf = cache_swa_k.reshape(depth, bd, wc, SW_KV)
    vbuf = cache_swa_v.reshape(depth, bd, wc, SW_KV)
    g_ffn1, g_mix, g_ffn2, g_on = r3(norm_ffn1), r3(norm_mix), r3(norm_ffn2), r3(dn_out_norm)
    g_qn = r3(jnp.tile(sw_q_norm * SW_SCALE, (1, SW_KV_HEADS)))
    g_kn = r3(jnp.tile(sw_k_norm, (1, SW_KV_HEADS)))

    xp = x_prompt.reshape(bp * seq, d)
    xs = x_sample.reshape(bd * dseq, d)
    ms = bd * dseq
    outs = [[] for _ in range(8)]
    for l in range(depth):
        xp = _ffn(xp, mod_p, 0, l, g_ffn1, w_ffn1_in, w_ffn1_out, TM_PROMPT, TF)
        yp = _inproj(xp, mod_p, l, g_mix, w_in_r, TM_PROMPT, TN)
        oa, s_p, conv_p = _gdn(yp, bp, l, dn_conv_w, alog_row, dt_row, g_on, None, None, DN_CHUNK,
                               GDN_CHUNKS_PER_STEP, 1, BF16)
        ob, k_p, v_p = _swa_prompt(yp, bp, l, g_qn, g_kn, sink_p, bias_p, col_q, BF16, SWA_Q_BLOCKS_PER_STEP)
        xp = _mixout(xp, oa, ob, yp, mod_p, l, w_dn16, w_sw16, w_o16, TM_MIX)
        xp = _ffn(xp, mod_p, 6, l, g_ffn2, w_ffn2_in, w_ffn2_out, TM_PROMPT, TF)
        xs = _ffn(xs, mod_s, 0, l, g_ffn1, w_ffn1_in, w_ffn1_out, ms, TF_SAMPLE)
        ys = _inproj(xs, mod_s, l, g_mix, w_in_r, ms, TN)
        oa, s_s, conv_s = _gdn(ys, bd, l, dn_conv_w, alog_row, dt_row, g_on, state_conv, state_delta, dseq, 1,
                               GDN_SAMPLE_SEQS_PER_STEP, F32)
        ob, k_s, v_s = _swa_sample(ys.reshape(bd, dseq, -1), l, kbuf, vbuf, g_qn, g_kn, sink_s, bias_sb, bias_sn,
                                   col_q, SAMPLE_BATCH_PER_STEP)
        xs = _mixout(xs, oa, ob.reshape(ms, SW_Q), ys, mod_s, l, w_dn16, w_sw16, w_o16, ms)
        xs = _ffn(xs, mod_s, 6, l, g_ffn2, w_ffn2_in, w_ffn2_out, ms, TF_SAMPLE)
        for lst, val in zip(outs, (s_p, conv_p, k_p, v_p, s_s, conv_s, k_s, v_s)):
            lst.append(val)
    s_p, conv_p, k_p, v_p, s_s, conv_s, k_s, v_s = (jnp.stack(o) for o in outs)
    kv = lambda a: a.reshape(a.shape[:3] + (SW_KV_HEADS, SW_HD))
    return (xp.reshape(bp, seq, d), xs.reshape(bd, dseq, d), s_p, conv_p, kv(k_p), kv(v_p),
            s_s, conv_s, kv(k_s), kv(v_s))
```

```python
import functools
import math

import numpy as np
import jax
import jax.numpy as jnp
from jax import lax
from jax.experimental import pallas as pl
from jax.experimental.pallas import tpu as pltpu

F32 = jnp.float32
BF16 = jnp.bfloat16

NORM_EPS = 1e-6
NEG_INF = -1e30
N_MOD = 9

DN_HEADS = 8
DN_DK = 128
DN_DV = 128
DN_CONV = 4
DN_CHUNK = 64
DN_QK = DN_HEADS * DN_DK
DN_V = DN_HEADS * DN_DV
DN_CONV_CH = 2 * DN_QK + DN_V
DN_SCALE = DN_DK ** -0.5

SW_HEADS = 16
SW_KV_HEADS = 4
SW_HD = 64
SW_GROUP = SW_HEADS // SW_KV_HEADS
SW_SCALE = SW_HD ** -0.5
WINDOW = 128
SW_BLOCK = 128
SW_Q = SW_HEADS * SW_HD
SW_KV = SW_KV_HEADS * SW_HD

N_BUCKETS = 32
MAX_DISTANCE = 128

LANES = 128
SUBLANES = 8
VMEM_LIMIT_BYTES = 56 * 1024 * 1024
LARGE_TILE_VMEM_LIMIT_BYTES = 60 * 1024 * 1024

COL_CONV = 0
COL_Z = COL_CONV + DN_CONV_CH
COL_GA = COL_Z + DN_V
BD_WIDTH = LANES


def _sigmoid(x):
    return 1.0 / (1.0 + jnp.exp(-x))


def _silu(x):
    return x * _sigmoid(x)


def _softplus(x):
    return jnp.maximum(x, 0.0) + jnp.log1p(jnp.exp(-jnp.abs(x)))


def _dot(a, b, **kw):
    return jnp.dot(a, b, preferred_element_type=F32, **kw)


def _dot_nt(a, b):
    return lax.dot_general(a, b, (((1,), (1,)), ((), ())), preferred_element_type=F32)


def _dot_tn(a, b):
    return lax.dot_general(a, b, (((0,), (0,)), ((), ())), preferred_element_type=F32)


def _params(*semantics, vmem_limit_bytes=VMEM_LIMIT_BYTES):
    return pltpu.CompilerParams(dimension_semantics=semantics, vmem_limit_bytes=vmem_limit_bytes)


def _mod_norm(x, gain, sc, sh):
    r = lax.rsqrt(jnp.mean(x * x, axis=-1, keepdims=True) + NORM_EPS)
    if sc.shape[0] == 1:
        return (x * r) * (gain * (1.0 + sc)) + sh
    return x * r * gain * (1.0 + sc) + sh


def _ada_kernel(c_ref, w_ref, b_ref, o_ref):
    a = _silu(c_ref[...]).astype(BF16)
    o_ref[...] = _dot(a, w_ref[...].astype(BF16)) + b_ref[...]


def _ada(c_all, w_ada, b_ada, tn=1024):
    depth, d, n = w_ada.shape
    rows = c_all.shape[0]
    return pl.pallas_call(
        _ada_kernel,
        grid=(depth, n // tn),
        in_specs=[
            pl.BlockSpec((rows, d), lambda l, j: (0, 0)),
            pl.BlockSpec((None, d, tn), lambda l, j: (l, 0, j)),
            pl.BlockSpec((None, 1, tn), lambda l, j: (l, 0, j)),
        ],
        out_specs=pl.BlockSpec((None, rows, tn), lambda l, j: (l, 0, j)),
        out_shape=jax.ShapeDtypeStruct((depth, rows, n), F32),
        compiler_params=_params("arbitrary", "arbitrary"),
        name="ada",
    )(c_all, w_ada, b_ada.reshape(depth, 1, n))


def _bucket_table():
    max_exact = N_BUCKETS // 2
    d = np.arange(WINDOW + 1)
    ratio = np.log(np.maximum(d, 1).astype(np.float32) / np.float32(max_exact)) / np.float32(math.log(MAX_DISTANCE / max_exact))
    large = max_exact + (ratio.astype(np.float32) * np.float32(N_BUCKETS - max_exact)).astype(np.int32)
    return np.where(d < max_exact, d, np.minimum(large, N_BUCKETS - 1)).astype(np.int32)


def _bucket_map(dist):
    table = _bucket_table()
    valid = (dist >= 0) & (dist <= WINDOW)
    return np.where(valid, table[np.clip(dist, 0, WINDOW)], -1).astype(np.int32)


def _bias_kernel(rb_ref, bm_ref, o_ref):
    bm = bm_ref[...]
    tq = bm.shape[0]
    for g in range(SW_KV_HEADS):
        for j in range(SW_GROUP):
            acc = jnp.full(bm.shape, NEG_INF, F32)
            for b in range(N_BUCKETS):
                acc = jnp.where(bm == b, rb_ref[b, g * SW_GROUP + j], acc)
            o_ref[g, j * tq:(j + 1) * tq, :] = acc


def _bias_grid(rel_bias, bmap):
    tq, tk = bmap.shape
    return pl.pallas_call(
        _bias_kernel,
        in_specs=[
            pl.BlockSpec(memory_space=pltpu.SMEM),
            pl.BlockSpec((tq, tk), lambda: (0, 0)),
        ],
        out_specs=pl.BlockSpec((SW_KV_HEADS, SW_GROUP * tq, tk), lambda: (0, 0, 0)),
        out_shape=jax.ShapeDtypeStruct((SW_KV_HEADS, SW_GROUP * tq, tk), F32),
        name="relbias",
    )(rel_bias, jnp.asarray(bmap))


def _ffn_kernel(*refs, nf):
    f = pl.program_id(1)
    _ffn_prologue(f, *refs)
    _ffn_body(f, nf, *refs)
    _ffn_epilogue(f, nf, *refs)


def _ffn_prologue(f, x_ref, sh_ref, sc_ref, g_ref, gain_ref, wg_ref, wu_ref, wo_ref, o_ref, h_ref):
    @pl.when(f == 0)
    def _():
        h_ref[...] = _mod_norm(x_ref[...], gain_ref[...], sc_ref[...], sh_ref[...]).astype(BF16)
        o_ref[...] = jnp.zeros_like(o_ref)


def _ffn_body(f, nf, x_ref, sh_ref, sc_ref, g_ref, gain_ref, wg_ref, wu_ref, wo_ref, o_ref, h_ref, pad_steps=False):
    h = h_ref[...]
    gate = _dot(h, wg_ref[...].astype(BF16))
    up = _dot(h, wu_ref[...].astype(BF16))
    a = (_silu(gate) * up).astype(BF16)
    update = _dot(a, wo_ref[...].astype(BF16))
    if pad_steps:
        update = jnp.where(f < nf, update, 0.0)
    o_ref[...] += update


def _ffn_epilogue(f, nf, x_ref, sh_ref, sc_ref, g_ref, gain_ref, wg_ref, wu_ref, wo_ref, o_ref, h_ref):
    @pl.when(f == nf - 1)
    def _():
        o_ref[...] = x_ref[...] + (0.5 * g_ref[...]) * o_ref[...]


def _mod_spec(mod, layer, seg, d, tiles_per_row):
    r = mod.shape[2]
    return pl.BlockSpec((None, None, r, d), lambda i, j: (layer, i // tiles_per_row, 0, seg))


def _ffn(x, mod, seg0, layer, gain, w_in, w_out, tm, tf):
    m, d = x.shape
    ff = w_out.shape[1]
    nf = ff // tf
    tiles_per_row = (m // mod.shape[1]) // tm
    return pl.pallas_call(
        functools.partial(_ffn_kernel, nf=nf),
        grid=(m // tm, nf),
        in_specs=[
            pl.BlockSpec((tm, d), lambda i, f: (i, 0)),
            _mod_spec(mod, layer, seg0, d, tiles_per_row),
            _mod_spec(mod, layer, seg0 + 1, d, tiles_per_row),
            _mod_spec(mod, layer, seg0 + 2, d, tiles_per_row),
            pl.BlockSpec((None, 1, d), lambda i, f: (layer, 0, 0)),
            pl.BlockSpec((None, d, tf), lambda i, f: (layer, 0, f)),
            pl.BlockSpec((None, d, tf), lambda i, f: (layer, 0, f + nf)),
            pl.BlockSpec((None, tf, d), lambda i, f: (layer, f, 0)),
        ],
        out_specs=pl.BlockSpec((tm, d), lambda i, f: (i, 0)),
        out_shape=jax.ShapeDtypeStruct((m, d), F32),
        scratch_shapes=[pltpu.VMEM((tm, d), BF16)],
        compiler_params=_params("arbitrary", "arbitrary", vmem_limit_bytes=LARGE_TILE_VMEM_LIMIT_BYTES),
        name="ffn",
    )(x, mod, mod, mod, gain, w_in, w_in, w_out)


def _inproj_kernel(x_ref, sh_ref, sc_ref, gain_ref, w_ref, o_ref, h_ref):
    @pl.when(pl.program_id(1) == 0)
    def _():
        h_ref[...] = _mod_norm(x_ref[...], gain_ref[...], sc_ref[...], sh_ref[...]).astype(BF16)

    o_ref[...] = _dot_nt(h_ref[...], w_ref[...])


def _inproj(x, mod, layer, gain, w_r, tm, tn):
    m, d = x.shape
    nr = w_r.shape[0]
    tiles_per_row = (m // mod.shape[1]) // tm
    return pl.pallas_call(
        _inproj_kernel,
        grid=(m // tm, nr // tn),
        in_specs=[
            pl.BlockSpec((tm, d), lambda i, j: (i, 0)),
            _mod_spec(mod, layer, 3, d, tiles_per_row),
            _mod_spec(mod, layer, 4, d, tiles_per_row),
            pl.BlockSpec((None, 1, d), lambda i, j: (layer, 0, 0)),
            pl.BlockSpec((tn, d), lambda i, j: (j, 0)),
        ],
        out_specs=pl.BlockSpec((tm, tn), lambda i, j: (i, j)),
        out_shape=jax.ShapeDtypeStruct((m, nr), F32),
        scratch_shapes=[pltpu.VMEM((tm, d), BF16)],
        compiler_params=_params("arbitrary", "arbitrary"),
        name="inproj",
    )(x, mod, mod, gain, w_r)


def _unit_lower_inverses(lmats, ri, ci):
    c = lmats[0].shape[0]
    n = len(lmats)
    eye = (ri == ci).astype(F32)
    if c == SUBLANES:
        ts = [eye for _ in lmats]
        for j in range(c - 1):
            ts = [t - l[:, j:j + 1] * t[j:j + 1, :] for t, l in zip(ts, lmats)]
        return ts
    pair = (jnp.right_shift(ri, 1) == jnp.right_shift(ci, 1)) & (ri == ci + 1)
    ts = [eye - jnp.where(pair, l, 0.0) for l in lmats]
    shift = 1
    while (1 << shift) < c:
        rb = jnp.right_shift(ri, shift)
        cb = jnp.right_shift(ci, shift)
        off = (jnp.right_shift(rb, 1) == jnp.right_shift(cb, 1)) & (rb == cb + 1)
        t16 = [t.astype(BF16) for t in ts]
        ps = [_dot(t16[h], jnp.where(off, lmats[h], 0.0).astype(BF16)).astype(BF16) for h in range(n)]
        xs = [_dot(ps[h], t16[h]) for h in range(n)]
        ts = [ts[h] - xs[h] for h in range(n)]
        shift += 1
    return ts


def _gdn_kernel(*refs, chunk, chunks_per_step, seqs, has_init, reorder, side_ffn):
    it = iter(refs)
    xc_ref, z_ref, bd_ref, cw_ref, alog_ref, dt_ref, on_ref = (next(it) for _ in range(7))
    conv0_ref, s0_ref = (next(it), next(it)) if has_init else (None, None)
    wt_ref = next(it) if reorder else None
    ffn_in = [next(it) for _ in range(8)] if side_ffn else None
    o_ref, s_ref, conv_ref = next(it), next(it), next(it)
    wr_ref = next(it) if reorder else None
    ffn_out = next(it) if side_ffn else None
    xs_ref = next(it)
    ffn_h = next(it) if side_ffn else None
    t = pl.program_id(1)
    nt = pl.num_programs(1)
    step = pl.program_id(0) * nt + t

    if reorder:
        @pl.when(step < reorder[1])
        def _():
            _reorder_kernel(wt_ref, wr_ref, segments=reorder[0])

    if side_ffn:
        _ffn_prologue(step, *ffn_in, ffn_out, ffn_h)

    c = chunk
    tb = c * chunks_per_step
    hist = SUBLANES
    seq_ids = range(seqs)

    @pl.when(t == 0)
    def _():
        for s in seq_ids:
            xs_ref[hist * s:hist * (s + 1), :] = jnp.zeros((hist, DN_CONV_CH), F32)
            if has_init:
                xs_ref[hist * (s + 1) - (DN_CONV - 1):hist * (s + 1), :] = conv0_ref[s]
        if has_init:
            s_ref[...] = s0_ref[...]
        else:
            s_ref[...] = jnp.zeros_like(s_ref)

    accs = []
    for s in seq_ids:
        x = xc_ref[tb * s:tb * (s + 1), :]
        acc = x * cw_ref[DN_CONV - 1:DN_CONV, :]
        shifted = jnp.concatenate([xs_ref[hist * s:hist * (s + 1), :], x], axis=0)
        for k in range(1, DN_CONV):
            shifted = pltpu.roll(shifted, 1, axis=0)
            acc = acc + shifted[hist:, :] * cw_ref[DN_CONV - 1 - k:DN_CONV - k, :]
        accs.append(acc)
        xs_ref[hist * s:hist * (s + 1), :] = x[tb - hist:, :]
    xc_all = _silu(accs[0] if seqs == 1 else jnp.concatenate(accs, axis=0))

    @pl.when(t == nt - 1)
    def _():
        for s in seq_ids:
            conv_ref[s] = xs_ref[hist * (s + 1) - (DN_CONV - 1):hist * (s + 1), :]

    bd = bd_ref[...]
    beta_blk = _sigmoid(bd)
    g_blk = -jnp.exp(alog_ref[...]) * _softplus(bd + dt_ref[...])
    ri = lax.broadcasted_iota(jnp.int32, (c, c), 0)
    ci = lax.broadcasted_iota(jnp.int32, (c, c), 1)
    incl = ri >= ci
    strict = ri > ci
    tri = incl.astype(F32)

    on = on_ref[...]
    heads = range(DN_HEADS)
    blocks = [(s, j) for s in seq_ids for j in range(chunks_per_step)]
    q_all = xc_all[:, 0:DN_QK]
    k_all = xc_all[:, DN_QK:2 * DN_QK]
    rows_all = seqs * tb
    if tb >= DN_CHUNK:
        q_scale = _seg_rms_scale(q_all, DN_HEADS, mean=False)
        k_scale = _seg_rms_scale(k_all, DN_HEADS, mean=False)
    else:
        q_scale, k_scale = (jnp.concatenate(
            [jnp.broadcast_to(lax.rsqrt(jnp.sum(jnp.square(a[:, DN_DK * h:DN_DK * (h + 1)]), axis=-1, keepdims=True)
                                        + NORM_EPS), (rows_all, DN_DK)) for h in heads], axis=-1) for a in (q_all, k_all))
    qn_all = q_all * (q_scale * DN_SCALE)
    kn_all = k_all * k_scale
    qs, ks, vs, betas, gcols, glasts, decays, egcs = [], [], [], [], [], [], [], []
    for s, j in blocks:
        rows = slice(tb * s + c * j, tb * s + c * (j + 1))
        xc = xc_all[rows, :]
        qn = qn_all[rows, :]
        kn = kn_all[rows, :]
        beta_all = beta_blk[rows, :]
        gc_all = _dot(tri, g_blk[rows, :], precision=lax.Precision.HIGHEST)
        gc_pad = gc_all if c == LANES else jnp.concatenate([gc_all, jnp.zeros((LANES - c, LANES), F32)], axis=0)
        gc_t = gc_pad.T
        for h in heads:
            qs.append(qn[:, DN_DK * h:DN_DK * (h + 1)])
            ks.append(kn[:, DN_DK * h:DN_DK * (h + 1)])
            vs.append(xc[:, 2 * DN_QK + DN_DV * h:2 * DN_QK + DN_DV * (h + 1)])
            betas.append(beta_all[:, h:h + 1])
            gcol = gc_all[:, DN_HEADS + h:DN_HEADS + h + 1]
            grow = gc_t[DN_HEADS + h:DN_HEADS + h + 1, 0:c]
            gcols.append(gcol)
            glasts.append(gc_all[c - 1:c, DN_HEADS + h:DN_HEADS + h + 1])
            decays.append(jnp.exp(jnp.where(incl, gcol - grow, NEG_INF)))
            egcs.append(jnp.exp(gcol))
    n = len(blocks) * DN_HEADS
    kbs = [ks[i] * betas[i] for i in range(n)]
    k16 = [ks[i].astype(BF16) for i in range(n)]
    kk = [_dot_nt(kbs[i].astype(BF16), k16[i]) for i in range(n)]
    qk = [_dot_nt(qs[i].astype(BF16), k16[i]) for i in range(n)]
    lmats = [jnp.where(strict, kk[i] * decays[i], 0.0) for i in range(n)]
    tinv = _unit_lower_inverses(lmats, ri, ci)
    rhs = [jnp.concatenate([vs[i] * betas[i], kbs[i] * egcs[i]], axis=-1).astype(BF16) for i in range(n)]
    sol = [_dot(tinv[i].astype(BF16), rhs[i]) for i in range(n)]
    wq = [jnp.concatenate([sol[i][:, DN_DV:], qs[i] * egcs[i]], axis=0).astype(BF16) for i in range(n)]
    a16 = [(qk[i] * decays[i]).astype(BF16) for i in range(n)]
    kdec = [(ks[i] * jnp.exp(glasts[i] - gcols[i])).astype(BF16) for i in range(n)]
    lanes = [(s, h) for s in seq_ids for h in heads]
    state = [s_ref[s, h] for s, h in lanes]
    for j in range(chunks_per_step):
        idx = [(s * chunks_per_step + j) * DN_HEADS + h for s, h in lanes]
        m = range(len(lanes))
        s16 = [st.astype(BF16) for st in state]
        wqs = [_dot(wq[idx[a]], s16[a]) for a in m]
        v16 = [(sol[idx[a]][:, :DN_DV] - wqs[a][:c, :]).astype(BF16) for a in m]
        av = [_dot(a16[idx[a]], v16[a]) for a in m]
        kv = [_dot_tn(kdec[idx[a]], v16[a]) for a in m]
        state = [state[a] * jnp.exp(glasts[idx[a]]) + kv[a] for a in m]
        for a, (s, h) in enumerate(lanes):
            o = wqs[a][c:, :] + av[a]
            o = o * lax.rsqrt(jnp.mean(o * o, axis=-1, keepdims=True) + NORM_EPS) * on
            rows = slice(tb * s + c * j, tb * s + c * (j + 1))
            zh = z_ref[rows, DN_DV * h:DN_DV * (h + 1)]
            o_ref[rows, DN_DV * h:DN_DV * (h + 1)] = (o * _silu(zh)).astype(o_ref.dtype)
    for a, (s, h) in enumerate(lanes):
        s_ref[s, h] = state[a]

    if side_ffn:
        _ffn_body(step, side_ffn, *ffn_in, ffn_out, ffn_h, pad_steps=True)
        _ffn_epilogue(step, side_ffn, *ffn_in, ffn_out, ffn_h)


def _gdn(y, batch, layer, conv_w, alog_row, dt_row, out_norm, conv0, s0, chunk, chunks_per_step, seqs, out_dtype,
         reorder_next=None, side_ffn=None):
    m = y.shape[0]
    seq = m // batch
    tb = chunk * chunks_per_step
    nt = seq // tb
    assert seqs == 1 or nt == 1
    has_init = conv0 is not None
    row = lambda b, t: b * nt + t
    rb = seqs * tb
    in_specs = [
        pl.BlockSpec((rb, DN_CONV_CH), lambda b, t: (row(b, t), COL_CONV // DN_CONV_CH)),
        pl.BlockSpec((rb, DN_V), lambda b, t: (row(b, t), COL_Z // DN_V)),
        pl.BlockSpec((rb, BD_WIDTH), lambda b, t: (row(b, t), y.shape[1] // BD_WIDTH - 1 - _BD_TAIL_BLOCKS)),
        pl.BlockSpec((None, DN_CONV, DN_CONV_CH), lambda b, t: (layer, 0, 0)),
        pl.BlockSpec((None, 1, LANES), lambda b, t: (layer, 0, 0)),
        pl.BlockSpec((None, 1, LANES), lambda b, t: (layer, 0, 0)),
        pl.BlockSpec((None, 1, DN_DV), lambda b, t: (layer, 0, 0)),
    ]
    args = [y, y, y, conv_w, alog_row, dt_row, out_norm]
    if has_init:
        in_specs += [
            pl.BlockSpec((None, seqs, DN_CONV - 1, DN_CONV_CH), lambda b, t: (layer, b, 0, 0)),
            pl.BlockSpec((None, seqs, DN_HEADS, DN_DK, DN_DV), lambda b, t: (layer, b, 0, 0, 0)),
        ]
        args += [conv0, s0]
    out_specs = [
        pl.BlockSpec((rb, DN_V), lambda b, t: (row(b, t), 0)),
        pl.BlockSpec((seqs, DN_HEADS, DN_DK, DN_DV), lambda b, t: (b, 0, 0, 0)),
        pl.BlockSpec((seqs, DN_CONV - 1, DN_CONV_CH), lambda b, t: (b, 0, 0)),
    ]
    out_shape = [
        jax.ShapeDtypeStruct((m, DN_V), out_dtype),
        jax.ShapeDtypeStruct((batch, DN_HEADS, DN_DK, DN_DV), F32),
        jax.ShapeDtypeStruct((batch, DN_CONV - 1, DN_CONV_CH), F32),
    ]
    reorder = None
    if reorder_next is not None:
        w_t, nxt = reorder_next
        segments, n_out = _reorder_segments(w_t.shape[2])
        n_slabs = w_t.shape[2] // LANES
        assert n_slabs <= (batch // seqs) * nt
        slab = lambda b, t: jnp.minimum(row(b, t), n_slabs - 1)
        in_specs.append(pl.BlockSpec((None, w_t.shape[1], LANES), lambda b, t: (nxt, 0, slab(b, t))))
        args.append(w_t)
        out_specs.append(pl.BlockSpec((n_out, LANES), lambda b, t: (0, slab(b, t))))
        out_shape.append(jax.ShapeDtypeStruct((n_out, w_t.shape[2]), BF16))
        reorder = (segments, n_slabs)
    scratch_shapes = [pltpu.VMEM((seqs * SUBLANES, DN_CONV_CH), F32)]
    nf = None
    if side_ffn is not None:
        fx, fmod, fseg, fgain, fw_in, fw_out, tf = side_ffn
        fm, d = fx.shape
        nf = fw_out.shape[1] // tf
        assert nf <= (batch // seqs) * nt and fmod.shape[1] == 1 and fmod.shape[2] == fm
        chunk_idx = lambda b, t: jnp.minimum(row(b, t), nf - 1)
        once = dict(pipeline_mode=pl.Buffered(1))
        in_specs += [pl.BlockSpec((fm, d), lambda b, t: (0, 0), **once)]
        in_specs += [pl.BlockSpec((None, None, fm, d), lambda b, t, k=k: (layer, 0, 0, fseg + k), **once) for k in range(3)]
        in_specs += [
            pl.BlockSpec((None, 1, d), lambda b, t: (layer, 0, 0)),
            pl.BlockSpec((None, d, tf), lambda b, t: (layer, 0, chunk_idx(b, t))),
            pl.BlockSpec((None, d, tf), lambda b, t: (layer, 0, chunk_idx(b, t) + nf)),
            pl.BlockSpec((None, tf, d), lambda b, t: (layer, chunk_idx(b, t), 0)),
        ]
        args += [fx, fmod, fmod, fmod, fgain, fw_in, fw_in, fw_out]
        out_specs.append(pl.BlockSpec((fm, d), lambda b, t: (0, 0), **once))
        out_shape.append(jax.ShapeDtypeStruct((fm, d), F32))
        scratch_shapes.append(pltpu.VMEM((fm, d), BF16))
    return pl.pallas_call(
        functools.partial(_gdn_kernel, chunk=chunk, chunks_per_step=chunks_per_step, seqs=seqs, has_init=has_init,
                          reorder=reorder, side_ffn=nf),
        grid=(batch // seqs, nt),
        in_specs=in_specs,
        out_specs=out_specs,
        out_shape=out_shape,
        scratch_shapes=scratch_shapes,
        compiler_params=_params("arbitrary", "arbitrary", vmem_limit_bytes=LARGE_TILE_VMEM_LIMIT_BYTES),
        name="gdn",
    )(*args)


def _seg_rms_scale(x, heads, mean=True):
    w = x.shape[1]
    hd = w // heads
    shift = int(math.log2(hd))
    norm = 1.0 / hd if mean else 1.0
    seg = (jnp.right_shift(lax.broadcasted_iota(jnp.int32, (w, LANES), 0), shift)
           == lax.broadcasted_iota(jnp.int32, (w, LANES), 1)).astype(BF16)
    seg_t = (lax.broadcasted_iota(jnp.int32, (LANES, w), 0)
             == jnp.right_shift(lax.broadcasted_iota(jnp.int32, (LANES, w), 1), shift)).astype(BF16)
    scale = lax.rsqrt(_dot((x * x).astype(BF16), seg) * norm + NORM_EPS)
    hi = scale.astype(BF16)
    lo = (scale - hi.astype(F32)).astype(BF16)
    return _dot(hi, seg_t) + _dot(lo, seg_t)


def _same_shape_fold(xs, op):
    acc = {}
    for x in xs:
        acc[x.shape] = x if x.shape not in acc else op(acc[x.shape], x)
    return list(acc.values())


def _attn_groups(groups):
    tq = groups[0][0][0].shape[0]
    qg = [jnp.concatenate(g[0], axis=0).astype(BF16) for g in groups]
    raw = [[_dot_nt(q, k) for k in g[1]] for q, g in zip(qg, groups)]
    probs, sink_terms = [], []
    for r, g in zip(raw, groups):
        sink = g[4]
        ss = [s + b for s, b in zip(r, g[3])]
        m = sink
        for s in _same_shape_fold(ss, jnp.maximum):
            m = jnp.maximum(m, jnp.max(s, axis=-1, keepdims=True))
        probs.append([jnp.exp(s - m).astype(BF16) for s in ss])
        sink_terms.append(jnp.exp(sink - m))
    pv = [[_dot(p, v) for p, v in zip(ps, g[2])] for ps, g in zip(probs, groups)]
    rowsum = [[_dot(p, jnp.ones((p.shape[1], LANES), BF16)) for p in ps] for ps in probs]
    outs = []
    for parts, sums, es in zip(pv, rowsum, sink_terms):
        o = parts[0]
        for part in parts[1:]:
            o = o + part
        den = sums[0]
        for part in sums[1:]:
            den = den + part
        o = o * (1.0 / (den + es))[:, :o.shape[1]]
        outs += [o[tq * j:tq * (j + 1), :] for j in range(SW_GROUP)]
    return outs


def _head_cols(x, heads):
    return [x[:, SW_HD * h:SW_HD * (h + 1)] for h in heads]


def _swa_prompt_kernel(*refs, qb, reorder):
    it = iter(refs)
    q_ref, kp_ref, kc_ref, vp_ref, vc_ref, kq_ref, kn_ref, sink_ref, bias_ref = (next(it) for _ in range(9))
    wt_ref = next(it) if reorder else None
    o_ref, kout_ref, vout_ref = next(it), next(it), next(it)
    wr_ref = next(it) if reorder else None
    n = pl.program_id(1)
    nb = pl.num_programs(1)

    if reorder:
        @pl.when(pl.program_id(0) * nb + n < reorder[1])
        def _():
            _reorder_kernel(wt_ref, wr_ref, segments=reorder[0])

    kv_heads = range(SW_KV_HEADS)
    q = q_ref[...]
    qn = q * _seg_rms_scale(q, SW_HEADS)
    kc = kc_ref[...]
    kc = kc * _seg_rms_scale(kc, SW_KV_HEADS) * kn_ref[...]
    kp = kp_ref[...]
    kp = kp * _seg_rms_scale(kp, SW_KV_HEADS) * kn_ref[...]
    k_all = jnp.concatenate([kp, kc], axis=0) * kq_ref[...]
    vc = vc_ref[...]
    v_all = jnp.concatenate([vp_ref[...], vc], axis=0)
    first = n == 0
    groups = []
    for s in range(qb):
        blk = lambda a, i: a[SW_BLOCK * i:SW_BLOCK * (i + 1), :]
        qs = blk(qn, s)
        ks = [_head_cols(blk(k_all, s + i), kv_heads) for i in range(2)]
        vs = [_head_cols(blk(v_all, s + i), kv_heads) for i in range(2)]
        for g in kv_heads:
            bias = bias_ref[g]
            bias_prev = bias[:, :SW_BLOCK]
            if s == 0:
                bias_prev = jnp.where(first, NEG_INF, bias_prev)
            groups.append((
                _head_cols(qs, range(SW_GROUP * g, SW_GROUP * (g + 1))),
                [ks[0][g].astype(BF16), ks[1][g].astype(BF16)],
                [vs[0][g].astype(BF16), vs[1][g].astype(BF16)],
                [bias_prev, bias[:, SW_BLOCK:]],
                sink_ref[g]))
    outs = _attn_groups(groups)
    for s in range(qb):
        o_ref[SW_BLOCK * s:SW_BLOCK * (s + 1), :] = jnp.concatenate(
            outs[SW_HEADS * s:SW_HEADS * (s + 1)], axis=-1).astype(o_ref.dtype)

    @pl.when(n == nb - 1)
    def _():
        kout_ref[...] = kc[SW_BLOCK * (qb - 1):, :]
        vout_ref[...] = vc[SW_BLOCK * (qb - 1):, :]


def _swa_prompt(y, batch, layer, q_norm, k_norm, sink_col, bias, col_q, out_dtype, qb, reorder_next=None):
    m = y.shape[0]
    seq = m // batch
    tq = SW_BLOCK * qb
    nb = seq // tq
    cq = col_q // SW_Q
    ck = (col_q + SW_Q) // SW_KV
    cv = ck + 1
    cur = lambda b, n: b * nb + n
    prev = lambda b, n: (b * nb + n) * qb - jnp.minimum(n, 1)
    extra_in, extra_args, extra_out, extra_shape, reorder = [], [], [], [], None
    if reorder_next is not None:
        w_t, nxt = reorder_next
        segments, n_out = _reorder_segments(w_t.shape[2])
        n_slabs = w_t.shape[2] // LANES
        assert n_slabs <= batch * nb
        slab = lambda b, n: jnp.minimum(cur(b, n), n_slabs - 1)
        extra_in = [pl.BlockSpec((None, w_t.shape[1], LANES), lambda b, n: (nxt, 0, slab(b, n)))]
        extra_args = [w_t]
        extra_out = [pl.BlockSpec((n_out, LANES), lambda b, n: (0, slab(b, n)))]
        extra_shape = [jax.ShapeDtypeStruct((n_out, w_t.shape[2]), BF16)]
        reorder = (segments, n_slabs)
    return pl.pallas_call(
        functools.partial(_swa_prompt_kernel, qb=qb, reorder=reorder),
        grid=(batch, nb),
        in_specs=[
            pl.BlockSpec((tq, SW_Q), lambda b, n: (cur(b, n), cq)),
            pl.BlockSpec((SW_BLOCK, SW_KV), lambda b, n: (prev(b, n), ck)),
            pl.BlockSpec((tq, SW_KV), lambda b, n: (cur(b, n), ck)),
            pl.BlockSpec((SW_BLOCK, SW_KV), lambda b, n: (prev(b, n), cv)),
            pl.BlockSpec((tq, SW_KV), lambda b, n: (cur(b, n), cv)),
            pl.BlockSpec((None, 1, SW_KV), lambda b, n: (layer, 0, 0)),
            pl.BlockSpec((None, 1, SW_KV), lambda b, n: (layer, 0, 0)),
            pl.BlockSpec((None, SW_KV_HEADS, SW_GROUP * SW_BLOCK, 1), lambda b, n: (layer, 0, 0, 0)),
            pl.BlockSpec((SW_KV_HEADS, SW_GROUP * SW_BLOCK, 2 * SW_BLOCK), lambda b, n: (0, 0, 0)),
        ] + extra_in,
        out_specs=[
            pl.BlockSpec((tq, SW_Q), lambda b, n: (cur(b, n), 0)),
            pl.BlockSpec((None, WINDOW, SW_KV), lambda b, n: (b, 0, 0)),
            pl.BlockSpec((None, WINDOW, SW_KV), lambda b, n: (b, 0, 0)),
        ] + extra_out,
        out_shape=[
            jax.ShapeDtypeStruct((m, SW_Q), out_dtype),
            jax.ShapeDtypeStruct((batch, WINDOW, SW_KV), F32),
            jax.ShapeDtypeStruct((batch, WINDOW, SW_KV), F32),
        ] + extra_shape,
        compiler_params=_params("arbitrary", "arbitrary"),
        name="swa_prompt",
    )(y, y, y, y, y, q_norm, k_norm, sink_col, bias, *extra_args)


def _swa_sample_kernel(q_ref, k_ref, v_ref, kbuf_ref, vbuf_ref, kq_ref, kn_ref, sink_ref, bb_ref, bn_ref,
                       o_ref, kout_ref, vout_ref, *, bs):
    t = q_ref.shape[1]
    wc = kbuf_ref.shape[1]
    kv_heads = range(SW_KV_HEADS)
    groups = []
    for i in range(bs):
        q = q_ref[i]
        qn = q * _seg_rms_scale(q, SW_HEADS)
        kn = k_ref[i]
        kn = kn * _seg_rms_scale(kn, SW_KV_HEADS) * kn_ref[...]
        v = v_ref[i]
        kbuf = kbuf_ref[i]
        vbuf = vbuf_ref[i]
        kbs = _head_cols(kbuf * kq_ref[...], kv_heads)
        kns = _head_cols(kn * kq_ref[...], kv_heads)
        vbs = _head_cols(vbuf, kv_heads)
        vns = _head_cols(v, kv_heads)
        for g in kv_heads:
            groups.append((
                _head_cols(qn, range(SW_GROUP * g, SW_GROUP * (g + 1))),
                [kbs[g].astype(BF16), kns[g].astype(BF16)],
                [vbs[g].astype(BF16), vns[g].astype(BF16)],
                [bb_ref[g], bn_ref[g]],
                sink_ref[g]))
        kout_ref[i, 0:wc - t, :] = kbuf[t:, :]
        kout_ref[i, wc - t:wc, :] = kn
        vout_ref[i, 0:wc - t, :] = vbuf[t:, :]
        vout_ref[i, wc - t:wc, :] = v
    outs = _attn_groups(groups)
    for i in range(bs):
        o_ref[i] = jnp.concatenate(outs[SW_HEADS * i:SW_HEADS * (i + 1)], axis=-1).astype(o_ref.dtype)


def _swa_sample(y3, layer, k_buf, v_buf, q_norm, k_norm, sink_col, bias_buf, bias_new, col_q, bs):
    batch, t, _ = y3.shape
    wc = k_buf.shape[2]
    cq = col_q // SW_Q
    ck = (col_q + SW_Q) // SW_KV
    cv = ck + 1
    return pl.pallas_call(
        functools.partial(_swa_sample_kernel, bs=bs),
        grid=(batch // bs,),
        in_specs=[
            pl.BlockSpec((bs, t, SW_Q), lambda b: (b, 0, cq)),
            pl.BlockSpec((bs, t, SW_KV), lambda b: (b, 0, ck)),
            pl.BlockSpec((bs, t, SW_KV), lambda b: (b, 0, cv)),
            pl.BlockSpec((None, bs, wc, SW_KV), lambda b: (layer, b, 0, 0)),
            pl.BlockSpec((None, bs, wc, SW_KV), lambda b: (layer, b, 0, 0)),
            pl.BlockSpec((None, 1, SW_KV), lambda b: (layer, 0, 0)),
            pl.BlockSpec((None, 1, SW_KV), lambda b: (layer, 0, 0)),
            pl.BlockSpec((None, SW_KV_HEADS, SW_GROUP * t, 1), lambda b: (layer, 0, 0, 0)),
            pl.BlockSpec((SW_KV_HEADS, SW_GROUP * t, wc), lambda b: (0, 0, 0)),
            pl.BlockSpec((SW_KV_HEADS, SW_GROUP * t, t), lambda b: (0, 0, 0)),
        ],
        out_specs=[
            pl.BlockSpec((bs, t, SW_Q), lambda b: (b, 0, 0)),
            pl.BlockSpec((bs, wc, SW_KV), lambda b: (b, 0, 0)),
            pl.BlockSpec((bs, wc, SW_KV), lambda b: (b, 0, 0)),
        ],
        out_shape=[
            jax.ShapeDtypeStruct((batch, t, SW_Q), F32),
            jax.ShapeDtypeStruct((batch, wc, SW_KV), F32),
            jax.ShapeDtypeStruct((batch, wc, SW_KV), F32),
        ],
        compiler_params=_params("arbitrary"),
        name="swa_sample",
    )(y3, y3, y3, k_buf, v_buf, q_norm, k_norm, sink_col, bias_buf, bias_new)


def _mixout_kernel(x_ref, oa_ref, ob_ref, ga_ref, gb_ref, g_ref, wdn_ref, wsw_ref, wo_ref, o_ref):
    ya = _dot(oa_ref[...].astype(BF16), wdn_ref[...])
    yb = _dot(ob_ref[...].astype(BF16), wsw_ref[...])
    merged = _sigmoid(ga_ref[...]) * ya + _sigmoid(gb_ref[...]) * yb
    o_ref[...] = x_ref[...] + g_ref[...] * _dot(merged.astype(BF16), wo_ref[...])


def _mixout(x, o_a, o_b, y, mod, layer, w_dn, w_sw, w_o, tm):
    m, d = x.shape
    tiles_per_row = (m // mod.shape[1]) // tm
    cga = COL_GA // d
    const = dict(pipeline_mode=pl.Buffered(1))
    return pl.pallas_call(
        _mixout_kernel,
        grid=(m // tm,),
        in_specs=[
            pl.BlockSpec((tm, d), lambda i: (i, 0)),
            pl.BlockSpec((tm, DN_V), lambda i: (i, 0)),
            pl.BlockSpec((tm, SW_Q), lambda i: (i, 0)),
            pl.BlockSpec((tm, d), lambda i: (i, cga)),
            pl.BlockSpec((tm, d), lambda i: (i, cga + 1)),
            pl.BlockSpec((None, None, mod.shape[2], d), lambda i: (layer, i // tiles_per_row, 0, 5)),
            pl.BlockSpec((None, DN_V, d), lambda i: (layer, 0, 0), **const),
            pl.BlockSpec((None, SW_Q, d), lambda i: (layer, 0, 0), **const),
            pl.BlockSpec((None, d, d), lambda i: (layer, 0, 0), **const),
        ],
        out_specs=pl.BlockSpec((tm, d), lambda i: (i, 0)),
        out_shape=jax.ShapeDtypeStruct((m, d), F32),
        compiler_params=_params("arbitrary", vmem_limit_bytes=LARGE_TILE_VMEM_LIMIT_BYTES),
        name="mixout",
    )(x, o_a, o_b, y, y, mod, w_dn, w_sw, w_o)


_BD_TAIL_BLOCKS = 3
TM_PROMPT = 1024
TF = 256
TF_SAMPLE = 512
TN = 1024
TM_MIX = 512
GDN_CHUNKS_PER_STEP = 4
GDN_SAMPLE_SEQS_PER_STEP = 4
SWA_Q_BLOCKS_PER_STEP = 2
SAMPLE_BATCH_PER_STEP = 8


def _reorder_kernel(w_ref, o_ref, *, segments):
    row = 0
    for start, height in segments:
        o_ref[row:row + height, :] = w_ref[start:start + height, :].astype(o_ref.dtype)
        row += height
    o_ref[row:, :] = jnp.zeros((o_ref.shape[0] - row, o_ref.shape[1]), o_ref.dtype)


def _reorder_segments(d):
    sizes = (DN_CONV_CH, DN_V, DN_HEADS, DN_HEADS, SW_Q, SW_KV, SW_KV, d, d)
    o = [int(v) for v in np.concatenate([[0], np.cumsum(sizes)])]
    segments = ((0, o[2]), (o[7], 2 * d), (o[4], o[7] - o[4]), (o[2], o[4] - o[2]))
    n_out = sum(h for _, h in segments) + BD_WIDTH - 2 * DN_HEADS + _BD_TAIL_BLOCKS * LANES
    return segments, n_out


def _reorder_layer(w_t, layer, tk=256):
    _, n_in, rows = w_t.shape
    segments, n_out = _reorder_segments(rows)
    return pl.pallas_call(
        functools.partial(_reorder_kernel, segments=segments),
        grid=(rows // tk,),
        in_specs=[pl.BlockSpec((None, n_in, tk), lambda i: (layer, 0, i))],
        out_specs=pl.BlockSpec((n_out, tk), lambda i: (0, i)),
        out_shape=jax.ShapeDtypeStruct((n_out, rows), BF16),
        compiler_params=_params("arbitrary"),
        name="reorder_w_in",
    )(w_t)


def kernel(x_prompt, x_sample, c_prompt, c_sample, state_delta, state_conv, cache_swa_k, cache_swa_v, rel_bias, w_ada, b_ada, norm_ffn1, w_ffn1_in, w_ffn1_out, norm_mix, w_in, dn_conv_w, dn_a_log, dn_dt_bias, dn_out_norm, w_dn_out, sw_q_norm, sw_k_norm, sw_sinks, w_sw_out, w_o, norm_ffn2, w_ffn2_in, w_ffn2_out):
    bp, seq, d = x_prompt.shape
    bd, dseq, _ = x_sample.shape
    depth = w_ada.shape[0]
    wc = cache_swa_k.shape[2]
    col_q = COL_GA + 2 * d

    n_rows = bp + bd
    pad_rows = -n_rows % SUBLANES
    c_all = jnp.concatenate([c_prompt, c_sample, jnp.zeros((pad_rows, d), F32)], axis=0)
    mod_all = _ada(c_all, w_ada, b_ada)
    mod_p = mod_all[:, :bp].reshape(depth, bp, 1, N_MOD * d)
    mod_s = jnp.repeat(mod_all[:, bp:n_rows], dseq, axis=1).reshape(depth, 1, bd * dseq, N_MOD * d)

    qi = np.arange(SW_BLOCK)[:, None]
    kj = np.arange(2 * SW_BLOCK)[None, :]
    bias_p = _bias_grid(rel_bias, _bucket_map(qi + SW_BLOCK - kj))
    ti = np.arange(dseq)[:, None]
    sj = np.arange(wc + dseq)[None, :]
    dist_s = ti + wc - sj
    bias_sb = _bias_grid(rel_bias, _bucket_map(dist_s[:, :wc]))
    bias_sn = _bias_grid(rel_bias, _bucket_map(dist_s[:, wc:]))

    w_in_t = jnp.swapaxes(w_in, 1, 2)
    w_in_r = _reorder_layer(w_in_t, 0)
    w_dn16 = w_dn_out.astype(BF16)
    w_sw16 = w_sw_out.astype(BF16)
    w_o16 = w_o.astype(BF16)
    r3 = lambda a: a.reshape(depth, 1, a.shape[-1])
    lane_row = lambda a: jnp.pad(a, ((0, 0), (DN_HEADS, LANES - 2 * DN_HEADS))).reshape(depth, 1, LANES)
    alog_row = lane_row(dn_a_log)
    dt_row = lane_row(dn_dt_bias)
    sinks = sw_sinks.reshape(depth, SW_KV_HEADS, SW_GROUP, 1)
    sink_p = jnp.repeat(sinks, SW_BLOCK, axis=2)
    sink_s = jnp.repeat(sinks, dseq, axis=2)
    kbuf = cache_swa_k.reshape(depth, bd, wc, SW_KV)
    vbuf = cache_swa_v.reshape(depth, bd, wc, SW_KV)
    g_ffn1, g_mix, g_ffn2, g_on = r3(norm_ffn1), r3(norm_mix), r3(norm_ffn2), r3(dn_out_norm)
    g_qn = r3(jnp.tile(sw_q_norm * SW_SCALE, (1, SW_KV_HEADS)))
    g_kn = r3(jnp.tile(sw_k_norm, (1, SW_KV_HEADS)))

    xp = x_prompt.reshape(bp * seq, d)
    xs = x_sample.reshape(bd * dseq, d)
    ms = bd * dseq
    outs = [[] for _ in range(8)]
    for l in range(depth):
        xp = _ffn(xp, mod_p, 0, l, g_ffn1, w_ffn1_in, w_ffn1_out, TM_PROMPT, TF)
        yp = _inproj(xp, mod_p, l, g_mix, w_in_r, TM_PROMPT, TN)
        oa, s_p, conv_p, xs = _gdn(yp, bp, l, dn_conv_w, alog_row, dt_row, g_on, None, None, DN_CHUNK,
                                   GDN_CHUNKS_PER_STEP, 1, BF16,
                                   side_ffn=(xs, mod_s, 0, g_ffn1, w_ffn1_in, w_ffn1_out, TF))
        nxt = (w_in_t, l + 1) if l + 1 < depth else None
        ob, k_p, v_p, *w_next = _swa_prompt(yp, bp, l, g_qn, g_kn, sink_p, bias_p, col_q, BF16,
                                            SWA_Q_BLOCKS_PER_STEP, reorder_next=nxt)
        xp = _mixout(xp, oa, ob, yp, mod_p, l, w_dn16, w_sw16, w_o16, TM_MIX)
        xp = _ffn(xp, mod_p, 6, l, g_ffn2, w_ffn2_in, w_ffn2_out, TM_PROMPT, TF)
        ys = _inproj(xs, mod_s, l, g_mix, w_in_r, ms, TN)
        oa, s_s, conv_s = _gdn(ys, bd, l, dn_conv_w, alog_row, dt_row, g_on, state_conv, state_delta, dseq, 1,
                               GDN_SAMPLE_SEQS_PER_STEP, F32)
        ob, k_s, v_s = _swa_sample(ys.reshape(bd, dseq, -1), l, kbuf, vbuf, g_qn, g_kn, sink_s, bias_sb, bias_sn,
                                   col_q, SAMPLE_BATCH_PER_STEP)
        xs = _mixout(xs, oa, ob.reshape(ms, SW_Q), ys, mod_s, l, w_dn16, w_sw16, w_o16, ms)
        xs = _ffn(xs, mod_s, 6, l, g_ffn2, w_ffn2_in, w_ffn2_out, ms, TF_SAMPLE)
        for lst, val in zip(outs, (s_p, conv_p, k_p, v_p, s_s, conv_s, k_s, v_s)):
            lst.append(val)
        if w_next:
            w_in_r = w_next[0]
    s_p, conv_p, k_p, v_p, s_s, conv_s, k_s, v_s = (jnp.stack(o) for o in outs)
    kv = lambda a: a.reshape(a.shape[:3] + (SW_KV_HEADS, SW_HD))
    return (xp.reshape(bp, seq, d), xs.reshape(bd, dseq, d), s_p, conv_p, kv(k_p), kv(v_p),
            s_s, conv_s, kv(k_s), kv(v_s))
```
